```python
import math
import jax, jax.numpy as jnp
from jax import lax
import numpy as np

D_MODEL = 1024
BATCH = 8
SEQ = 2048
DEPTH = 4
DEC_BATCH = 32
DEC_SEQ = 1
PAST_LEN = 8192
PAGE_SIZE = 128

N_A = DEPTH // 2
N_B = DEPTH - N_A
CONV_CH = 3 * D_MODEL // 4
CONV_WIDTH = 31
SB_HEAD_DIM = 64
SB_HEADS = CONV_CH // SB_HEAD_DIM
SB_WIDTH = SB_HEADS * SB_HEAD_DIM
SB_BIAS_INIT = -6.0
MEM_HEADS = 4
MEM_HEAD_DIM = 64
MEM_WIDTH = MEM_HEADS * MEM_HEAD_DIM
N_MEM = 256
MIX_WIDTH = CONV_CH + MEM_WIDTH
D_FF = ((8 * D_MODEL + 3 * 256 - 1) // (3 * 256)) * 256
Q_BLOCK = 128
RMS_EPS = 1e-6
LN_EPS = 1e-5

kernel_name = "yoco_conformer_stickbreaking_decode_step"


def rms_norm(x, g):
    xf = x.astype(jnp.float32)
    y = xf * lax.rsqrt(jnp.mean(xf * xf, axis=-1, keepdims=True) + RMS_EPS) * g.astype(jnp.float32)
    return y.astype(x.dtype)


def conv_module(a, gate, state, w, b, ln_g, ln_b):
    u = a * jax.nn.sigmoid(gate)
    ext = jnp.concatenate([state.astype(u.dtype), u], axis=1)
    new_state = ext[:, ext.shape[1] - (CONV_WIDTH - 1):]
    y = lax.conv_general_dilated(ext, w.astype(ext.dtype)[:, None, :], window_strides=(1,),
                                 padding='VALID', dimension_numbers=('NWC', 'WIO', 'NWC'),
                                 feature_group_count=u.shape[-1])
    y = y.astype(jnp.float32) + b.astype(jnp.float32)
    mu = jnp.mean(y, axis=-1, keepdims=True)
    var = jnp.mean(jnp.square(y - mu), axis=-1, keepdims=True)
    y = (y - mu) * lax.rsqrt(var + LN_EPS) * ln_g.astype(jnp.float32) + ln_b.astype(jnp.float32)
    y = y * jax.nn.sigmoid(y)
    return y.astype(u.dtype), new_state


def stick_breaking_attention(q, k, v, bias, q_offset):
    B, Tq, H, dh = q.shape
    Tk = k.shape[1]
    blk = min(Tq, Q_BLOCK)
    nb = -(-Tq // blk)
    pad = nb * blk - Tq
    qp = jnp.pad(q, ((0, 0), (0, pad), (0, 0), (0, 0)))
    qb = qp.reshape(B, nb, blk, H, dh).transpose(1, 0, 2, 3, 4)
    kf = k.astype(jnp.float32)
    vf = v.astype(jnp.float32)
    bf = bias.astype(jnp.float32)[None, :, None, None]
    k_pos = jnp.arange(Tk)
    scale = 1.0 / math.sqrt(dh)

    def one_block(args):
        qblk, start = args
        z = jnp.einsum('bqhd,bkhd->bhqk', qblk.astype(jnp.float32), kf) * scale + bf
        q_pos = q_offset + start + jnp.arange(blk)
        causal = k_pos[None, :] < q_pos[:, None]
        log_keep = jnp.where(causal, jax.nn.log_sigmoid(-z), 0.0)
        after = lax.cumsum(log_keep, axis=3, reverse=True) - log_keep
        w = jnp.where(causal, jnp.exp(jax.nn.log_sigmoid(z) + after), 0.0)
        return jnp.einsum('bhqk,bkhd->bqhd', w, vf)

    out = lax.map(one_block, (qb, jnp.arange(nb) * blk))
    out = out.transpose(1, 0, 2, 3, 4).reshape(B, nb * blk, H, dh)[:, :Tq]
    return out.astype(q.dtype)


def memory_attention(q, mk, mv):
    s = jnp.einsum('bthd,bmhd->bhtm', q.astype(jnp.float32), mk.astype(jnp.float32)) / math.sqrt(MEM_HEAD_DIM)
    p = jax.nn.softmax(s, axis=-1)
    return jnp.einsum('bhtm,bmhd->bthd', p, mv.astype(jnp.float32)).astype(q.dtype)


def run_trunk(x, mem_k, mem_v, conv_state, past_k, past_v,
              norm_mix_pre, norm_mix_post, norm_ffn_pre, norm_ffn_post,
              w_in_a, conv_w, conv_b, conv_ln_g, conv_ln_b, w_in_b, sb_bias, kv_norm_g, w_kv,
              w_out, w_ffn_up, w_ffn_down):
    B, T, _ = x.shape
    past = past_k.shape[1]
    new_conv = []
    k_new = v_new = k_all = v_all = None
    for l in range(DEPTH):
        h = rms_norm(x, norm_mix_pre[l])
        if l < N_A:
            proj = h @ w_in_a[l]
            a, gate, q_mem = jnp.split(proj, [CONV_CH, 2 * CONV_CH], axis=-1)
            mix, st = conv_module(a, gate, conv_state[l], conv_w[l], conv_b[l], conv_ln_g[l], conv_ln_b[l])
            new_conv.append(st)
        else:
            proj = h @ w_in_b[l - N_A]
            q_sb, q_mem = jnp.split(proj, [SB_WIDTH], axis=-1)
            mix = stick_breaking_attention(q_sb.reshape(B, T, SB_HEADS, SB_HEAD_DIM), k_all, v_all,
                                           sb_bias[l - N_A], past)
            mix = mix.reshape(B, T, SB_WIDTH)
        mo = memory_attention(q_mem.reshape(B, T, MEM_HEADS, MEM_HEAD_DIM), mem_k[l], mem_v[l]).reshape(B, T, MEM_WIDTH)
        x = x + rms_norm(jnp.concatenate([mix, mo], axis=-1) @ w_out[l], norm_mix_post[l])
        h = rms_norm(x, norm_ffn_pre[l])
        g, u = jnp.split(h @ w_ffn_up[l], 2, axis=-1)
        x = x + rms_norm((jax.nn.silu(g) * u) @ w_ffn_down[l], norm_ffn_post[l])
        if l == N_A - 1:
            kv = rms_norm(x, kv_norm_g) @ w_kv
            k_new, v_new = jnp.split(kv, 2, axis=-1)
            k_new = k_new.reshape(B, T, SB_HEADS, SB_HEAD_DIM)
            v_new = v_new.reshape(B, T, SB_HEADS, SB_HEAD_DIM)
            k_all = jnp.concatenate([past_k.astype(k_new.dtype), k_new], axis=1)
            v_all = jnp.concatenate([past_v.astype(v_new.dtype), v_new], axis=1)
    return x, jnp.stack(new_conv), k_new, v_new


def setup_inputs(seed: int = 0) -> dict:
    key = jax.random.key(seed)
    ks = jax.random.split(key, 32)
    n_pages = PAST_LEN // PAGE_SIZE
    n_phys = (DEC_BATCH * n_pages * 5 + 3) // 4
    f32 = jnp.float32

    def nrm(k, shape, scale=1.0):
        return jax.random.normal(k, shape, f32) * scale

    def gain(k, shape):
        return 1.0 + 0.05 * jax.random.normal(k, shape, f32)

    perm = jax.random.permutation(ks[0], n_phys)
    page_table = perm[:DEC_BATCH * n_pages].reshape(DEC_BATCH, n_pages).astype(jnp.int32)
    return {
        'x_prompt': nrm(ks[1], (BATCH, SEQ, D_MODEL)),
        'x_sample': nrm(ks[2], (DEC_BATCH, DEC_SEQ, D_MODEL)),
        'cache_k': nrm(ks[3], (n_phys, PAGE_SIZE, SB_HEADS, SB_HEAD_DIM)),
        'cache_v': nrm(ks[4], (n_phys, PAGE_SIZE, SB_HEADS, SB_HEAD_DIM)),
        'state_conv': nrm(ks[5], (N_A, DEC_BATCH, CONV_WIDTH - 1, CONV_CH), 0.5),
        'cache_mem_k': nrm(ks[6], (DEPTH, DEC_BATCH, N_MEM, MEM_HEADS, MEM_HEAD_DIM)),
        'cache_mem_v': nrm(ks[7], (DEPTH, DEC_BATCH, N_MEM, MEM_HEADS, MEM_HEAD_DIM)),
        'page_table': page_table,
        'mem_prompt': nrm(ks[8], (BATCH, N_MEM, D_MODEL)),
        'norm_mix_pre': gain(ks[9], (DEPTH, D_MODEL)),
        'norm_mix_post': gain(ks[10], (DEPTH, D_MODEL)),
        'norm_ffn_pre': gain(ks[11], (DEPTH, D_MODEL)),
        'norm_ffn_post': gain(ks[12], (DEPTH, D_MODEL)),
        'w_in_a': nrm(ks[13], (N_A, D_MODEL, 2 * CONV_CH + MEM_WIDTH), D_MODEL ** -0.5),
        'conv_w': nrm(ks[14], (N_A, CONV_WIDTH, CONV_CH), CONV_WIDTH ** -0.5),
        'conv_b': nrm(ks[15], (N_A, CONV_CH), 0.02),
        'conv_ln_g': gain(ks[16], (N_A, CONV_CH)),
        'conv_ln_b': nrm(ks[17], (N_A, CONV_CH), 0.02),
        'w_in_b': nrm(ks[18], (N_B, D_MODEL, SB_WIDTH + MEM_WIDTH), D_MODEL ** -0.5),
        'sb_bias': SB_BIAS_INIT + nrm(ks[26], (N_B, SB_HEADS), 0.1),
        'kv_norm_g': gain(ks[19], (D_MODEL,)),
        'w_kv': nrm(ks[20], (D_MODEL, 2 * SB_WIDTH), D_MODEL ** -0.5),
        'mem_norm_g': gain(ks[21], (DEPTH, D_MODEL)),
        'w_mem_kv': nrm(ks[22], (DEPTH, D_MODEL, 2 * MEM_WIDTH), D_MODEL ** -0.5),
        'w_out': nrm(ks[23], (DEPTH, MIX_WIDTH, D_MODEL), MIX_WIDTH ** -0.5),
        'w_ffn_up': nrm(ks[24], (DEPTH, D_MODEL, 2 * D_FF), D_MODEL ** -0.5),
        'w_ffn_down': nrm(ks[25], (DEPTH, D_FF, D_MODEL), D_FF ** -0.5),
    }


def reference(x_prompt, x_sample, cache_k, cache_v, state_conv, cache_mem_k, cache_mem_v,
              page_table, mem_prompt, norm_mix_pre, norm_mix_post, norm_ffn_pre, norm_ffn_post,
              w_in_a, conv_w, conv_b, conv_ln_g, conv_ln_b, w_in_b, sb_bias, kv_norm_g, w_kv,
              mem_norm_g, w_mem_kv, w_out, w_ffn_up, w_ffn_down):
    memn = rms_norm(mem_prompt[None], mem_norm_g[:, None, None, :])
    mkv = jnp.einsum('lbmd,lde->lbme', memn, w_mem_kv)
    mk_p, mv_p = jnp.split(mkv, 2, axis=-1)
    mk_p = mk_p.reshape(DEPTH, BATCH, N_MEM, MEM_HEADS, MEM_HEAD_DIM)
    mv_p = mv_p.reshape(DEPTH, BATCH, N_MEM, MEM_HEADS, MEM_HEAD_DIM)
    zero_conv = jnp.zeros((N_A, BATCH, CONV_WIDTH - 1, CONV_CH), x_prompt.dtype)
    empty = jnp.zeros((BATCH, 0, SB_HEADS, SB_HEAD_DIM), x_prompt.dtype)
    y_prompt, conv_prompt, k_prompt, v_prompt = run_trunk(
        x_prompt, mk_p, mv_p, zero_conv, empty, empty,
        norm_mix_pre, norm_mix_post, norm_ffn_pre, norm_ffn_post,
        w_in_a, conv_w, conv_b, conv_ln_g, conv_ln_b, w_in_b, sb_bias, kv_norm_g, w_kv,
        w_out, w_ffn_up, w_ffn_down)
    n_pages = page_table.shape[1]
    past_k = cache_k[page_table].reshape(DEC_BATCH, n_pages * PAGE_SIZE, SB_HEADS, SB_HEAD_DIM)
    past_v = cache_v[page_table].reshape(DEC_BATCH, n_pages * PAGE_SIZE, SB_HEADS, SB_HEAD_DIM)
    y_sample, conv_sample, k_sample, v_sample = run_trunk(
        x_sample, cache_mem_k, cache_mem_v, state_conv, past_k, past_v,
        norm_mix_pre, norm_mix_post, norm_ffn_pre, norm_ffn_post,
        w_in_a, conv_w, conv_b, conv_ln_g, conv_ln_b, w_in_b, sb_bias, kv_norm_g, w_kv,
        w_out, w_ffn_up, w_ffn_down)
    return (y_prompt, y_sample, k_prompt, v_prompt, conv_prompt, mk_p, mv_p, k_sample, v_sample, conv_sample)
```

```python
import functools
import math

import jax
import jax.numpy as jnp
from jax import lax
from jax.experimental import pallas as pl
from jax.experimental.pallas import tpu as pltpu

F32 = jnp.float32
BF16 = jnp.bfloat16

D_MODEL = 1024
BATCH = 8
SEQ = 2048
DEPTH = 4
DEC_BATCH = 32
PAST_LEN = 8192
PAGE_SIZE = 128
N_PAGES = PAST_LEN // PAGE_SIZE
N_A = DEPTH // 2
CONV_CH = 3 * D_MODEL // 4
CONV_WIDTH = 31
SB_HEAD_DIM = 64
SB_HEADS = CONV_CH // SB_HEAD_DIM
SB_WIDTH = SB_HEADS * SB_HEAD_DIM
MEM_HEADS = 4
MEM_HEAD_DIM = 64
MEM_WIDTH = MEM_HEADS * MEM_HEAD_DIM
N_MEM = 256
D_FF = ((8 * D_MODEL + 3 * 256 - 1) // (3 * 256)) * 256
RMS_EPS = 1e-6
LN_EPS = 1e-5
SB_SCALE = 1.0 / math.sqrt(SB_HEAD_DIM)
MEM_SCALE = 1.0 / math.sqrt(MEM_HEAD_DIM)

N_PROMPT = BATCH * SEQ

V7X_VMEM_BYTES = 64 * 1024 * 1024
LANES = 128
SUBLANES = 8

ROW_TILE = 512
CONV_HALO = 32
CONV_ROWS = 32
SB_TQ = 256
SB_TK = 256
PAGES_PER_STEP = 8


def _params(vmem_mib, semantics):
    return pltpu.CompilerParams(
        dimension_semantics=semantics,
        vmem_limit_bytes=min(vmem_mib * 1024 * 1024, V7X_VMEM_BYTES - 8 * 1024 * 1024),
    )


def _rms(x, g):
    return x * lax.rsqrt(jnp.mean(x * x, axis=-1, keepdims=True) + RMS_EPS) * g


def _layer_norm_swish(y, ln_g, ln_b):
    mu = jnp.mean(y, axis=-1, keepdims=True)
    d = y - mu
    var = jnp.mean(d * d, axis=-1, keepdims=True)
    y = d * lax.rsqrt(var + LN_EPS) * ln_g + ln_b
    return y * jax.nn.sigmoid(y)


def _softplus(z):
    return jnp.maximum(z, 0.0) + jnp.log(1.0 + jnp.exp(-jnp.abs(z)))


def _split_bf16(x):
    hi = x.astype(BF16)
    lo = (x - hi.astype(F32)).astype(BF16)
    return hi, lo


def _mem_attn(qm, mk, mv):
    head_of_lane = lax.broadcasted_iota(jnp.int32, mk.shape, 1) // MEM_HEAD_DIM
    zero = jnp.zeros_like(mk)
    out = None
    for h in range(MEM_HEADS):
        in_head = head_of_lane == h
        kh = jnp.where(in_head, mk, zero)
        s = lax.dot_general(qm, kh, (((1,), (1,)), ((), ())), preferred_element_type=F32)
        s = s - jnp.max(s, axis=-1, keepdims=True)
        p = jnp.exp(s)
        p = p / jnp.sum(p, axis=-1, keepdims=True)
        vh = jnp.where(in_head, mv, zero)
        o = jnp.dot(p.astype(BF16), vh, preferred_element_type=F32)
        out = o if out is None else out + o
    return out


def _memkv_kernel(m_ref, g_ref, w_ref, k_ref, v_ref):
    h = _rms(m_ref[...], g_ref[0]).astype(BF16)
    kv = jnp.dot(h, w_ref[0].astype(BF16), preferred_element_type=F32)
    k_ref[0] = kv[:, :MEM_WIDTH]
    v_ref[0] = kv[:, MEM_WIDTH:]


def _memkv(mem_flat, mem_norm_g, w_mem_kv):
    n = mem_flat.shape[0]
    tm = ROW_TILE
    out = jax.ShapeDtypeStruct((DEPTH, n, MEM_WIDTH), F32)
    return pl.pallas_call(
        _memkv_kernel,
        grid=(DEPTH, n // tm),
        in_specs=[
            pl.BlockSpec((tm, D_MODEL), lambda l, i: (i, 0)),
            pl.BlockSpec((1, 1, D_MODEL), lambda l, i: (l, 0, 0)),
            pl.BlockSpec((1, D_MODEL, 2 * MEM_WIDTH), lambda l, i: (l, 0, 0)),
        ],
        out_specs=[
            pl.BlockSpec((1, tm, MEM_WIDTH), lambda l, i: (l, i, 0)),
            pl.BlockSpec((1, tm, MEM_WIDTH), lambda l, i: (l, i, 0)),
        ],
        out_shape=[out, out],
        compiler_params=_params(32, ("arbitrary", "arbitrary")),
        name="memkv",
    )(mem_flat, mem_norm_g.reshape(DEPTH, 1, D_MODEL), w_mem_kv)


def _inproj_kernel(x_ref, g_ref, w_ref, main_ref, qm_ref, *, glu):
    h = _rms(x_ref[...], g_ref[...]).astype(BF16)
    proj = jnp.dot(h, w_ref[...], preferred_element_type=F32)
    if glu:
        a = proj[:, :CONV_CH]
        gate = proj[:, CONV_CH:2 * CONV_CH]
        main_ref[...] = (a * jax.nn.sigmoid(gate)).astype(main_ref.dtype)
        qm = proj[:, 2 * CONV_CH:]
    else:
        main_ref[...] = (proj[:, :SB_WIDTH] * SB_SCALE).astype(main_ref.dtype)
        qm = proj[:, SB_WIDTH:]
    qm_ref[...] = (qm * MEM_SCALE).astype(qm_ref.dtype)


def _inproj(x, g, w_bf16, *, glu, main_dtype, qm_dtype):
    n = x.shape[0]
    tm = min(ROW_TILE, n)
    width = w_bf16.shape[1]
    main_w = CONV_CH if glu else SB_WIDTH
    return pl.pallas_call(
        functools.partial(_inproj_kernel, glu=glu),
        grid=(n // tm,),
        in_specs=[
            pl.BlockSpec((tm, D_MODEL), lambda i: (i, 0)),
            pl.BlockSpec((1, D_MODEL), lambda i: (0, 0)),
            pl.BlockSpec((D_MODEL, width), lambda i: (0, 0)),
        ],
        out_specs=[
            pl.BlockSpec((tm, main_w), lambda i: (i, 0)),
            pl.BlockSpec((tm, MEM_WIDTH), lambda i: (i, 0)),
        ],
        out_shape=[
            jax.ShapeDtypeStruct((n, main_w), main_dtype),
            jax.ShapeDtypeStruct((n, MEM_WIDTH), qm_dtype),
        ],
        compiler_params=_params(40, ("arbitrary",)),
        name="inproj_a" if glu else "inproj_b",
    )(x, g.reshape(1, D_MODEL), w_bf16)


def _out_proj_residual(x, mix_bf16, mo, wo_ref, gp_ref):
    y = jnp.dot(mix_bf16, wo_ref[0:CONV_CH, :], preferred_element_type=F32)
    y = y + jnp.dot(mo.astype(BF16), wo_ref[CONV_CH:, :], preferred_element_type=F32)
    return x + _rms(y, gp_ref[...])


def _mix_conv_kernel(x_ref, u_ref, halo_ref, qm_ref, mk_ref, mv_ref, cw_ref, cb_ref, lg_ref,
                     lb_ref, wo_ref, gp_ref, o_ref, ext_ref, shift_ref, mix_ref, *, tt,
                     tiles_per_seq):
    first = (pl.program_id(0) % tiles_per_seq) == 0
    ext_ref[0:CONV_HALO, :] = jnp.where(first, 0.0, halo_ref[...])
    ext_ref[CONV_HALO:, :] = u_ref[...]
    lead = CONV_HALO - (CONV_WIDTH - 1)
    for k in range(1, SUBLANES):
        shift_ref[k - 1] = ext_ref[k:k + shift_ref.shape[1], :]

    def chunk(r, carry):
        base = pl.multiple_of(r * CONV_ROWS, CONV_ROWS)
        acc = jnp.broadcast_to(cb_ref[...], (CONV_ROWS, CONV_CH))
        for j in range(CONV_WIDTH):
            whole, k = divmod(lead + j, SUBLANES)
            rows = pl.ds(base + whole * SUBLANES, CONV_ROWS)
            window = ext_ref[rows, :] if k == 0 else shift_ref[k - 1, rows, :]
            acc = acc + window * cw_ref[j:j + 1, :]
        y = _layer_norm_swish(acc, lg_ref[...], lb_ref[...])
        mix_ref[pl.ds(base, CONV_ROWS), :] = y.astype(BF16)
        return carry

    lax.fori_loop(0, tt // CONV_ROWS, chunk, 0)
    mo = _mem_attn(qm_ref[...], mk_ref[...].astype(BF16), mv_ref[...].astype(BF16))
    o_ref[...] = _out_proj_residual(x_ref[...], mix_ref[...], mo, wo_ref, gp_ref)


def _mix_given_kernel(x_ref, mix_ref, qm_ref, mk_ref, mv_ref, wo_ref, gp_ref, o_ref):
    mo = _mem_attn(qm_ref[...], mk_ref[...].astype(BF16), mv_ref[...].astype(BF16))
    o_ref[...] = _out_proj_residual(x_ref[...], mix_ref[...], mo, wo_ref, gp_ref)


def _mix_prompt(x, main, qm, mk_all, mv_all, layer, wo_bf16, g_post, conv_params=None):
    n = x.shape[0]
    tt = ROW_TILE
    tps = SEQ // tt
    row = lambda i: (i, 0)
    const = lambda i: (0, 0)
    mem_idx = lambda i: (layer * BATCH + i // tps, 0)
    x_spec = pl.BlockSpec((tt, D_MODEL), row)
    main_spec = pl.BlockSpec((tt, CONV_CH), row)
    qm_spec = pl.BlockSpec((tt, MEM_WIDTH), row)
    mem_spec = pl.BlockSpec((N_MEM, MEM_WIDTH), mem_idx)
    wo_spec = pl.BlockSpec((D_MODEL, D_MODEL), const)
    gp_spec = pl.BlockSpec((1, D_MODEL), const)
    ch_spec = pl.BlockSpec((1, CONV_CH), const)
    if conv_params is not None:
        cw, cb, lg, lb = conv_params
        halo_blocks = tt // CONV_HALO
        halo_spec = pl.BlockSpec((CONV_HALO, CONV_CH),
                                 lambda i: (jnp.maximum(i * halo_blocks - 1, 0), 0))
        return pl.pallas_call(
            functools.partial(_mix_conv_kernel, tt=tt, tiles_per_seq=tps),
            grid=(n // tt,),
            in_specs=[x_spec, main_spec, halo_spec, qm_spec, mem_spec, mem_spec,
                      pl.BlockSpec((CONV_WIDTH, CONV_CH), const), ch_spec, ch_spec, ch_spec,
                      wo_spec, gp_spec],
            out_specs=x_spec,
            out_shape=jax.ShapeDtypeStruct((n, D_MODEL), F32),
            scratch_shapes=[pltpu.VMEM((tt + CONV_HALO, CONV_CH), F32),
                            pltpu.VMEM((SUBLANES - 1, tt + CONV_HALO - SUBLANES, CONV_CH), F32),
                            pltpu.VMEM((tt, CONV_CH), BF16)],
            compiler_params=_params(48, ("arbitrary",)),
            name="mix_conv",
        )(x, main, main, qm, mk_all, mv_all, cw, cb.reshape(1, CONV_CH), lg.reshape(1, CONV_CH),
          lb.reshape(1, CONV_CH), wo_bf16, g_post.reshape(1, D_MODEL))
    return pl.pallas_call(
        _mix_given_kernel,
        grid=(n // tt,),
        in_specs=[x_spec, main_spec, qm_spec, mem_spec, mem_spec, wo_spec, gp_spec],
        out_specs=x_spec,
        out_shape=jax.ShapeDtypeStruct((n, D_MODEL), F32),
        compiler_params=_params(40, ("arbitrary",)),
        name="mix_given",
    )(x, main, qm, mk_all, mv_all, wo_bf16, g_post.reshape(1, D_MODEL))


def _sample_mem_attn(qm_ref, mk_ref, mv_ref, mo_ref):
    def body(b, carry):
        q = jnp.broadcast_to(qm_ref[pl.ds(b, 1), :], (SUBLANES, MEM_WIDTH)).astype(BF16)
        o = _mem_attn(q, mk_ref[b].astype(BF16), mv_ref[b].astype(BF16))
        mo_ref[pl.ds(b, 1), :] = o[0:1, :]
        return carry

    lax.fori_loop(0, DEC_BATCH, body, 0)


def _mix_sample_conv_kernel(x_ref, u_ref, st_ref, qm_ref, mk_ref, mv_ref, cw_ref, cb_ref, lg_ref,
                            lb_ref, wo_ref, gp_ref, o_ref, nst_ref, y_ref, mo_ref):
    hist = CONV_WIDTH - 1
    w_hist = cw_ref[0:hist, :]
    for b in range(DEC_BATCH):
        st = st_ref[b]
        u_row = u_ref[b:b + 1, :]
        y_ref[b:b + 1, :] = (jnp.sum(st * w_hist, axis=0, keepdims=True)
                             + u_row * cw_ref[hist:hist + 1, :])
        nst_ref[b, 0:hist - 1, :] = st[1:hist, :]
        nst_ref[b, hist - 1:hist, :] = u_row
    mix = _layer_norm_swish(y_ref[...] + cb_ref[...], lg_ref[...], lb_ref[...])
    _sample_mem_attn(qm_ref, mk_ref, mv_ref, mo_ref)
    o_ref[...] = _out_proj_residual(x_ref[...], mix.astype(BF16), mo_ref[...], wo_ref, gp_ref)


def _mix_sample_given_kernel(x_ref, mix_ref, qm_ref, mk_ref, mv_ref, wo_ref, gp_ref, o_ref, mo_ref):
    _sample_mem_attn(qm_ref, mk_ref, mv_ref, mo_ref)
    o_ref[...] = _out_proj_residual(x_ref[...], mix_ref[...].astype(BF16), mo_ref[...], wo_ref,
                                    gp_ref)


def _full(shape):
    return pl.BlockSpec(shape, lambda i: (0,) * len(shape))


def _mix_sample(x, main, qm, mk, mv, wo_bf16, g_post, conv_params=None, state=None):
    nb = DEC_BATCH
    common_in = [_full((nb, MEM_WIDTH)), _full((nb, N_MEM, MEM_WIDTH)),
                 _full((nb, N_MEM, MEM_WIDTH))]
    tail_in = [_full((D_MODEL, D_MODEL)), _full((1, D_MODEL))]
    x_spec = _full((nb, D_MODEL))
    main_spec = _full((nb, CONV_CH))
    mo_scratch = pltpu.VMEM((nb, MEM_WIDTH), F32)
    if conv_params is not None:
        cw, cb, lg, lb = conv_params
        hist = CONV_WIDTH - 1
        ch = _full((1, CONV_CH))
        return pl.pallas_call(
            _mix_sample_conv_kernel,
            grid=(1,),
            in_specs=[x_spec, main_spec, _full((nb, hist, CONV_CH))] + common_in
                     + [_full((CONV_WIDTH, CONV_CH)), ch, ch, ch] + tail_in,
            out_specs=[x_spec, _full((nb, hist, CONV_CH))],
            out_shape=[jax.ShapeDtypeStruct((nb, D_MODEL), F32),
                       jax.ShapeDtypeStruct((nb, hist, CONV_CH), F32)],
            scratch_shapes=[pltpu.VMEM((nb, CONV_CH), F32), mo_scratch],
            compiler_params=_params(56, ("arbitrary",)),
            name="mix_sample_conv",
        )(x, main, state, qm, mk, mv, cw, cb.reshape(1, CONV_CH), lg.reshape(1, CONV_CH),
          lb.reshape(1, CONV_CH), wo_bf16, g_post.reshape(1, D_MODEL))
    return pl.pallas_call(
        _mix_sample_given_kernel,
        grid=(1,),
        in_specs=[x_spec, main_spec] + common_in + tail_in,
        out_specs=x_spec,
        out_shape=jax.ShapeDtypeStruct((nb, D_MODEL), F32),
        scratch_shapes=[mo_scratch],
        compiler_params=_params(56, ("arbitrary",)),
        name="mix_sample_given",
    )(x, main, qm, mk, mv, wo_bf16, g_post.reshape(1, D_MODEL))


def _ffn_kernel(x_ref, g1_ref, wup_ref, wdn_ref, g2_ref, o_ref):
    x = x_ref[...]
    h = _rms(x, g1_ref[...]).astype(BF16)
    g = jnp.dot(h, wup_ref[:, :D_FF], preferred_element_type=F32)
    u = jnp.dot(h, wup_ref[:, D_FF:], preferred_element_type=F32)
    a = (g * jax.nn.sigmoid(g) * u).astype(BF16)
    y = jnp.dot(a, wdn_ref[...], preferred_element_type=F32)
    o_ref[...] = x + _rms(y, g2_ref[...])


def _ffn(x, g_pre, wup_bf16, wdn_bf16, g_post):
    n = x.shape[0]
    tm = min(ROW_TILE, n)
    const = lambda i: (0, 0)
    single = pl.Buffered(1)
    return pl.pallas_call(
        _ffn_kernel,
        grid=(n // tm,),
        in_specs=[
            pl.BlockSpec((tm, D_MODEL), lambda i: (i, 0)),
            pl.BlockSpec((1, D_MODEL), const),
            pl.BlockSpec((D_MODEL, 2 * D_FF), const, pipeline_mode=single),
            pl.BlockSpec((D_FF, D_MODEL), const, pipeline_mode=single),
            pl.BlockSpec((1, D_MODEL), const),
        ],
        out_specs=pl.BlockSpec((tm, D_MODEL), lambda i: (i, 0)),
        out_shape=jax.ShapeDtypeStruct((n, D_MODEL), F32),
        compiler_params=_params(56, ("arbitrary",)),
        name="ffn",
    )(x, g_pre.reshape(1, D_MODEL), wup_bf16, wdn_bf16, g_post.reshape(1, D_MODEL))


def _kv_kernel(x_ref, g_ref, w_ref, k_ref, v_ref, kb_ref, vb_ref):
    h = _rms(x_ref[...], g_ref[...]).astype(BF16)
    kv = jnp.dot(h, w_ref[...], preferred_element_type=F32)
    k = kv[:, :SB_WIDTH]
    v = kv[:, SB_WIDTH:]
    k_ref[...] = k
    v_ref[...] = v
    kb_ref[...] = k.astype(BF16)
    vb_ref[...] = v.astype(BF16)


def _kv(x, g, w_bf16):
    n = x.shape[0]
    tm = min(ROW_TILE, n)
    const = lambda i: (0, 0)
    row = lambda i: (i, 0)
    o32 = jax.ShapeDtypeStruct((n, SB_WIDTH), F32)
    o16 = jax.ShapeDtypeStruct((n, SB_WIDTH), BF16)
    blk = pl.BlockSpec((tm, SB_WIDTH), row)
    return pl.pallas_call(
        _kv_kernel,
        grid=(n // tm,),
        in_specs=[pl.BlockSpec((tm, D_MODEL), row), pl.BlockSpec((1, D_MODEL), const),
                  pl.BlockSpec((D_MODEL, 2 * SB_WIDTH), const)],
        out_specs=[blk, blk, blk, blk],
        out_shape=[o32, o32, o16, o16],
        compiler_params=_params(40, ("arbitrary",)),
        name="kv_proj",
    )(x, g.reshape(1, D_MODEL), w_bf16)


def _sb_prompt_kernel(bias_ref, q_ref, k_ref, v_ref, o_ref, *, tq, tk):
    pair = pl.program_id(1)
    qi = pl.program_id(2)
    q = q_ref[...]
    low_q = lax.broadcasted_iota(jnp.int32, q.shape, 1) < SB_HEAD_DIM
    zero_q = jnp.zeros_like(q)
    q_heads = (jnp.where(low_q, q, zero_q), jnp.where(low_q, zero_q, q))
    biases = (bias_ref[2 * pair], bias_ref[2 * pair + 1])
    low_kv = lax.broadcasted_iota(jnp.int32, (tk, LANES), 1) < SB_HEAD_DIM
    zero_kv = jnp.zeros((tk, LANES), BF16)
    jj = lax.broadcasted_iota(jnp.int32, (tk, tk), 0)
    ss = lax.broadcasted_iota(jnp.int32, (tk, tk), 1)
    later = jnp.where(jj > ss, 1.0, 0.0).astype(BF16)

    def tile(kb, carries, acc, diagonal):
        start = pl.multiple_of(kb * tk, tk)
        k_t = k_ref[pl.ds(start, tk), :]
        v_t = v_ref[pl.ds(start, tk), :]
        v_heads = (jnp.where(low_kv, v_t, zero_kv), jnp.where(low_kv, zero_kv, v_t))
        new_carries = []
        for e in range(2):
            z = lax.dot_general(q_heads[e], k_t, (((1,), (1,)), ((), ())),
                                preferred_element_type=F32) + biases[e]
            sp = _softplus(z)
            log_beta = z - sp
            if diagonal:
                rr = lax.broadcasted_iota(jnp.int32, (tq, tk), 0)
                cc = lax.broadcasted_iota(jnp.int32, (tq, tk), 1)
                causal = cc < rr
                sp = jnp.where(causal, sp, 0.0)
            hi, lo = _split_bf16(sp)
            after = (jnp.dot(hi, later, preferred_element_type=F32)
                     + jnp.dot(lo, later, preferred_element_type=F32))
            w = jnp.exp(log_beta - after - carries[e])
            if diagonal:
                w = jnp.where(causal, w, 0.0)
            acc = acc + jnp.dot(w.astype(BF16), v_heads[e], preferred_element_type=F32)
            new_carries.append(carries[e] + jnp.sum(sp, axis=-1, keepdims=True))
        return tuple(new_carries), acc

    zero_carry = jnp.zeros((tq, 1), F32)
    carries, acc = tile(qi, (zero_carry, zero_carry), jnp.zeros((tq, LANES), F32), True)

    def body(t, state):
        c0, c1, a = state
        (c0, c1), a = tile(qi - 1 - t, (c0, c1), a, False)
        return c0, c1, a

    _, _, acc = lax.fori_loop(0, qi, body, (carries[0], carries[1], acc))
    o_ref[...] = acc.astype(o_ref.dtype)


def _sb_prompt(q_bf16, k_bf16, v_bf16, bias):
    tq, tk = SB_TQ, SB_TK
    nq = SEQ // tq
    pairs = SB_HEADS // 2
    return pl.pallas_call(
        functools.partial(_sb_prompt_kernel, tq=tq, tk=tk),
        grid=(BATCH, pairs, nq),
        in_specs=[
            pl.BlockSpec(memory_space=pltpu.SMEM),
            pl.BlockSpec((tq, LANES), lambda b, p, i: (b * nq + i, p)),
            pl.BlockSpec((SEQ, LANES), lambda b, p, i: (b, p)),
            pl.BlockSpec((SEQ, LANES), lambda b, p, i: (b, p)),
        ],
        out_specs=pl.BlockSpec((tq, LANES), lambda b, p, i: (b * nq + i, p)),
        out_shape=jax.ShapeDtypeStruct((N_PROMPT, SB_WIDTH), BF16),
        compiler_params=_params(40, ("arbitrary", "arbitrary", "arbitrary")),
        name="sb_prompt",
    )(bias, q_bf16, k_bf16, v_bf16)


def _sb_sample_kernel(pt_ref, q_ref, bias_ref, *refs):
    del pt_ref
    npg = PAGES_PER_STEP
    k_refs = refs[:npg]
    v_refs = refs[npg:2 * npg]
    o_ref = refs[2 * npg]
    acc_ref, carry_ref = refs[2 * npg + 1:]
    j = pl.program_id(1)

    @pl.when(j == 0)
    def _():
        acc_ref[...] = jnp.zeros_like(acc_ref)
        carry_ref[...] = jnp.zeros_like(carry_ref)

    q = q_ref[0]
    bias = bias_ref[...]
    ch = lax.broadcasted_iota(jnp.int32, (SB_WIDTH, LANES), 0) // SB_HEAD_DIM
    hd = lax.broadcasted_iota(jnp.int32, (SB_WIDTH, LANES), 1)
    head_sel = jnp.where(ch == hd, 1.0, 0.0).astype(BF16)
    ch_t = lax.broadcasted_iota(jnp.int32, (LANES, SB_WIDTH), 1) // SB_HEAD_DIM
    hd_t = lax.broadcasted_iota(jnp.int32, (LANES, SB_WIDTH), 0)
    head_sel_t = jnp.where(ch_t == hd_t, 1.0, 0.0).astype(BF16)
    ss = lax.broadcasted_iota(jnp.int32, (PAGE_SIZE, PAGE_SIZE), 0)
    jj = lax.broadcasted_iota(jnp.int32, (PAGE_SIZE, PAGE_SIZE), 1)
    later = jnp.where(jj > ss, 1.0, 0.0).astype(BF16)

    acc = acc_ref[...]
    carry = carry_ref[0:1, :]
    for p in range(npg):
        k_pg = k_refs[p][0]
        v_pg = v_refs[p][0]
        hi, lo = _split_bf16(k_pg * q)
        z = (jnp.dot(hi, head_sel, preferred_element_type=F32)
             + jnp.dot(lo, head_sel, preferred_element_type=F32)) + bias
        sp = _softplus(z)
        log_beta = z - sp
        sp_hi, sp_lo = _split_bf16(sp)
        after = (jnp.dot(later, sp_hi, preferred_element_type=F32)
                 + jnp.dot(later, sp_lo, preferred_element_type=F32))
        w = jnp.exp(log_beta - after - carry)
        carry = carry + jnp.sum(sp, axis=0, keepdims=True)
        w_wide = jnp.dot(w.astype(BF16), head_sel_t, preferred_element_type=F32)
        wv = w_wide * v_pg
        acc = acc + jnp.sum(wv.reshape(PAGE_SIZE // SUBLANES, SUBLANES, SB_WIDTH), axis=0)
    acc_ref[...] = acc
    carry_ref[...] = jnp.broadcast_to(carry, carry_ref.shape)

    @pl.when(j == pl.num_programs(1) - 1)
    def _():
        o_ref[0] = jnp.sum(acc, axis=0, keepdims=True)


def _sb_sample(q, bias, cache_k, cache_v, page_table):
    npg = PAGES_PER_STEP
    steps = N_PAGES // npg
    bias_row = jnp.zeros((1, LANES), F32).at[0, :SB_HEADS].set(bias)

    def page_spec(p):
        return pl.BlockSpec(
            (1, PAGE_SIZE, SB_WIDTH),
            lambda b, j, pt: (pt[b, N_PAGES - 1 - (j * npg + p)], 0, 0))

    grid_spec = pltpu.PrefetchScalarGridSpec(
        num_scalar_prefetch=1,
        grid=(DEC_BATCH, steps),
        in_specs=[pl.BlockSpec((1, 1, SB_WIDTH), lambda b, j, pt: (b, 0, 0)),
                  pl.BlockSpec((1, LANES), lambda b, j, pt: (0, 0))]
                 + [page_spec(p) for p in range(npg)] + [page_spec(p) for p in range(npg)],
        out_specs=pl.BlockSpec((1, 1, SB_WIDTH), lambda b, j, pt: (b, 0, 0)),
        scratch_shapes=[pltpu.VMEM((SUBLANES, SB_WIDTH), F32), pltpu.VMEM((SUBLANES, LANES), F32)],
    )
    out = pl.pallas_call(
        _sb_sample_kernel,
        grid_spec=grid_spec,
        out_shape=jax.ShapeDtypeStruct((DEC_BATCH, 1, SB_WIDTH), F32),
        compiler_params=_params(48, ("arbitrary", "arbitrary")),
        name="sb_sample",
    )(page_table, q.reshape(DEC_BATCH, 1, SB_WIDTH), bias_row,
      *([cache_k] * npg), *([cache_v] * npg))
    return out.reshape(DEC_BATCH, SB_WIDTH)


def kernel(x_prompt, x_sample, cache_k, cache_v, state_conv, cache_mem_k, cache_mem_v, page_table, mem_prompt, norm_mix_pre, norm_mix_post, norm_ffn_pre, norm_ffn_post, w_in_a, conv_w, conv_b, conv_ln_g, conv_ln_b, w_in_b, sb_bias, kv_norm_g, w_kv, mem_norm_g, w_mem_kv, w_out, w_ffn_up, w_ffn_down):
    hist = CONV_WIDTH - 1
    mk_p, mv_p = _memkv(mem_prompt.reshape(BATCH * N_MEM, D_MODEL), mem_norm_g, w_mem_kv)
    mk_flat = mk_p.reshape(DEPTH * BATCH * N_MEM, MEM_WIDTH)
    mv_flat = mv_p.reshape(DEPTH * BATCH * N_MEM, MEM_WIDTH)
    cache_k3 = cache_k.reshape(cache_k.shape[0], PAGE_SIZE, SB_WIDTH)
    cache_v3 = cache_v.reshape(cache_v.shape[0], PAGE_SIZE, SB_WIDTH)

    xp = x_prompt.reshape(N_PROMPT, D_MODEL)
    xs = x_sample.reshape(DEC_BATCH, D_MODEL)
    conv_p, conv_s = [], []
    k_p = v_p = kb_p = vb_p = k_s = v_s = None
    for l in range(DEPTH):
        wo = w_out[l].astype(BF16)
        mk_s = cache_mem_k[l].reshape(DEC_BATCH, N_MEM, MEM_WIDTH)
        mv_s = cache_mem_v[l].reshape(DEC_BATCH, N_MEM, MEM_WIDTH)
        if l < N_A:
            w_in = w_in_a[l].astype(BF16)
            conv_params = (conv_w[l], conv_b[l], conv_ln_g[l], conv_ln_b[l])
            u_p, qm_p = _inproj(xp, norm_mix_pre[l], w_in, glu=True, main_dtype=F32, qm_dtype=BF16)
            u_s, qm_s = _inproj(xs, norm_mix_pre[l], w_in, glu=True, main_dtype=F32, qm_dtype=F32)
            xp = _mix_prompt(xp, u_p, qm_p, mk_flat, mv_flat, l, wo, norm_mix_post[l], conv_params)
            xs, st = _mix_sample(xs, u_s, qm_s, mk_s, mv_s, wo, norm_mix_post[l], conv_params,
                                 state_conv[l])
            conv_p.append(u_p.reshape(BATCH, SEQ, CONV_CH)[:, SEQ - hist:])
            conv_s.append(st)
        else:
            w_in = w_in_b[l - N_A].astype(BF16)
            bias = sb_bias[l - N_A]
            q_p, qm_p = _inproj(xp, norm_mix_pre[l], w_in, glu=False, main_dtype=BF16, qm_dtype=BF16)
            q_s, qm_s = _inproj(xs, norm_mix_pre[l], w_in, glu=False, main_dtype=F32, qm_dtype=F32)
            mix_p = _sb_prompt(q_p, kb_p, vb_p, bias)
            mix_s = _sb_sample(q_s, bias, cache_k3, cache_v3, page_table)
            xp = _mix_prompt(xp, mix_p, qm_p, mk_flat, mv_flat, l, wo, norm_mix_post[l])
            xs = _mix_sample(xs, mix_s, qm_s, mk_s, mv_s, wo, norm_mix_post[l])
        wup = w_ffn_up[l].astype(BF16)
        wdn = w_ffn_down[l].astype(BF16)
        xp = _ffn(xp, norm_ffn_pre[l], wup, wdn, norm_ffn_post[l])
        xs = _ffn(xs, norm_ffn_pre[l], wup, wdn, norm_ffn_post[l])
        if l == N_A - 1:
            wkv = w_kv.astype(BF16)
            k_p, v_p, kb_p, vb_p = _kv(xp, kv_norm_g, wkv)
            k_s, v_s, _, _ = _kv(xs, kv_norm_g, wkv)

    mem_shape = (DEPTH, BATCH, N_MEM, MEM_HEADS, MEM_HEAD_DIM)
    return (
        xp.reshape(BATCH, SEQ, D_MODEL),
        xs.reshape(DEC_BATCH, 1, D_MODEL),
        k_p.reshape(BATCH, SEQ, SB_HEADS, SB_HEAD_DIM),
        v_p.reshape(BATCH, SEQ, SB_HEADS, SB_HEAD_DIM),
        jnp.stack(conv_p),
        mk_p.reshape(mem_shape),
        mv_p.reshape(mem_shape),
        k_s.reshape(DEC_BATCH, 1, SB_HEADS, SB_HEAD_DIM),
        v_s.reshape(DEC_BATCH, 1, SB_HEADS, SB_HEAD_DIM),
        jnp.stack(conv_s),
    )
```

```python
import functools
import math

import jax
import jax.numpy as jnp
from jax import lax
from jax.experimental import pallas as pl
from jax.experimental.pallas import tpu as pltpu

F32 = jnp.float32
BF16 = jnp.bfloat16

D_MODEL = 1024
BATCH = 8
SEQ = 2048
DEPTH = 4
DEC_BATCH = 32
PAST_LEN = 8192
PAGE_SIZE = 128
N_PAGES = PAST_LEN // PAGE_SIZE
N_A = DEPTH // 2
CONV_CH = 3 * D_MODEL // 4
CONV_WIDTH = 31
SB_HEAD_DIM = 64
SB_HEADS = CONV_CH // SB_HEAD_DIM
SB_WIDTH = SB_HEADS * SB_HEAD_DIM
MEM_HEADS = 4
MEM_HEAD_DIM = 64
MEM_WIDTH = MEM_HEADS * MEM_HEAD_DIM
N_MEM = 256
D_FF = ((8 * D_MODEL + 3 * 256 - 1) // (3 * 256)) * 256
RMS_EPS = 1e-6
LN_EPS = 1e-5
SB_SCALE = 1.0 / math.sqrt(SB_HEAD_DIM)
MEM_SCALE = 1.0 / math.sqrt(MEM_HEAD_DIM)

N_PROMPT = BATCH * SEQ

V7X_VMEM_BYTES = 64 * 1024 * 1024
LANES = 128
SUBLANES = 8

ROW_TILE = 512
CONV_HALO = 32
CONV_ROWS = 32
SB_TQ = 256
SB_TK = 256
PAGES_PER_STEP = 8
SB_ROWS_PER_PAGE = 16
MASKED_LOG = -1e30
SB_BIAS_PARTS = 3
LOG2E = 1.4426950408889634


def _params(vmem_mib, semantics):
    return pltpu.CompilerParams(
        dimension_semantics=semantics,
        vmem_limit_bytes=min(vmem_mib * 1024 * 1024, V7X_VMEM_BYTES - 8 * 1024 * 1024),
    )


def _rms(x, g):
    return x * lax.rsqrt(jnp.mean(x * x, axis=-1, keepdims=True) + RMS_EPS) * g


def _layer_norm_swish(y, ln_g, ln_b):
    mu = jnp.mean(y, axis=-1, keepdims=True)
    d = y - mu
    var = jnp.mean(d * d, axis=-1, keepdims=True)
    y = d * lax.rsqrt(var + LN_EPS) * ln_g + ln_b
    return y * jax.nn.sigmoid(y)


def _softplus(z):
    return jnp.maximum(z, 0.0) + jnp.log(1.0 + jnp.exp(-jnp.abs(z)))


def _split_bf16(x):
    hi = x.astype(BF16)
    lo = (x - hi.astype(F32)).astype(BF16)
    return hi, lo


def _mem_attn(qm, mk, mv):
    head_of_lane = lax.broadcasted_iota(jnp.int32, mk.shape, 1) // MEM_HEAD_DIM
    zero = jnp.zeros_like(mk)
    out = None
    for h in range(MEM_HEADS):
        in_head = head_of_lane == h
        kh = jnp.where(in_head, mk, zero)
        s = lax.dot_general(qm, kh, (((1,), (1,)), ((), ())), preferred_element_type=F32)
        s = s - jnp.max(s, axis=-1, keepdims=True)
        p = jnp.exp(s)
        p = p / jnp.sum(p, axis=-1, keepdims=True)
        vh = jnp.where(in_head, mv, zero)
        o = jnp.dot(p.astype(BF16), vh, preferred_element_type=F32)
        out = o if out is None else out + o
    return out


def _memkv_kernel(m_ref, g_ref, w_ref, k_ref, v_ref):
    h = _rms(m_ref[...], g_ref[0]).astype(BF16)
    kv = jnp.dot(h, w_ref[0].astype(BF16), preferred_element_type=F32)
    k_ref[0] = kv[:, :MEM_WIDTH]
    v_ref[0] = kv[:, MEM_WIDTH:]


def _memkv(mem_flat, mem_norm_g, w_mem_kv):
    n = mem_flat.shape[0]
    tm = ROW_TILE
    out = jax.ShapeDtypeStruct((DEPTH, n, MEM_WIDTH), F32)
    return pl.pallas_call(
        _memkv_kernel,
        grid=(DEPTH, n // tm),
        in_specs=[
            pl.BlockSpec((tm, D_MODEL), lambda l, i: (i, 0)),
            pl.BlockSpec((1, 1, D_MODEL), lambda l, i: (l, 0, 0)),
            pl.BlockSpec((1, D_MODEL, 2 * MEM_WIDTH), lambda l, i: (l, 0, 0)),
        ],
        out_specs=[
            pl.BlockSpec((1, tm, MEM_WIDTH), lambda l, i: (l, i, 0)),
            pl.BlockSpec((1, tm, MEM_WIDTH), lambda l, i: (l, i, 0)),
        ],
        out_shape=[out, out],
        compiler_params=_params(32, ("arbitrary", "arbitrary")),
        name="memkv",
    )(mem_flat, mem_norm_g.reshape(DEPTH, 1, D_MODEL), w_mem_kv)


def _inproj_kernel(x_ref, g_ref, w_ref, main_ref, qm_ref, *, glu):
    h = _rms(x_ref[...], g_ref[...]).astype(BF16)
    proj = jnp.dot(h, w_ref[...], preferred_element_type=F32)
    if glu:
        a = proj[:, :CONV_CH]
        gate = proj[:, CONV_CH:2 * CONV_CH]
        main_ref[...] = (a * jax.nn.sigmoid(gate)).astype(main_ref.dtype)
        qm = proj[:, 2 * CONV_CH:]
    else:
        main_ref[...] = (proj[:, :SB_WIDTH] * SB_SCALE).astype(main_ref.dtype)
        qm = proj[:, SB_WIDTH:]
    qm_ref[...] = (qm * MEM_SCALE).astype(qm_ref.dtype)


def _inproj(x, g, w_bf16, *, glu, main_dtype, qm_dtype):
    n = x.shape[0]
    tm = min(ROW_TILE, n)
    width = w_bf16.shape[1]
    main_w = CONV_CH if glu else SB_WIDTH
    return pl.pallas_call(
        functools.partial(_inproj_kernel, glu=glu),
        grid=(n // tm,),
        in_specs=[
            pl.BlockSpec((tm, D_MODEL), lambda i: (i, 0)),
            pl.BlockSpec((1, D_MODEL), lambda i: (0, 0)),
            pl.BlockSpec((D_MODEL, width), lambda i: (0, 0)),
        ],
        out_specs=[
            pl.BlockSpec((tm, main_w), lambda i: (i, 0)),
            pl.BlockSpec((tm, MEM_WIDTH), lambda i: (i, 0)),
        ],
        out_shape=[
            jax.ShapeDtypeStruct((n, main_w), main_dtype),
            jax.ShapeDtypeStruct((n, MEM_WIDTH), qm_dtype),
        ],
        compiler_params=_params(40, ("arbitrary",)),
        name="inproj_a" if glu else "inproj_b",
    )(x, g.reshape(1, D_MODEL), w_bf16)


def _out_proj_residual(x, mix_bf16, mo, wo_ref, gp_ref):
    y = jnp.dot(mix_bf16, wo_ref[0:CONV_CH, :], preferred_element_type=F32)
    y = y + jnp.dot(mo.astype(BF16), wo_ref[CONV_CH:, :], preferred_element_type=F32)
    return x + _rms(y, gp_ref[...])


def _mix_conv_kernel(x_ref, u_ref, halo_ref, qm_ref, mk_ref, mv_ref, cw_ref, cb_ref, lg_ref,
                     lb_ref, wo_ref, gp_ref, o_ref, ext_ref, shift_ref, mix_ref, *, tt,
                     tiles_per_seq):
    first = (pl.program_id(0) % tiles_per_seq) == 0
    ext_ref[0:CONV_HALO, :] = jnp.where(first, 0.0, halo_ref[...])
    ext_ref[CONV_HALO:, :] = u_ref[...]
    lead = CONV_HALO - (CONV_WIDTH - 1)
    for k in range(1, SUBLANES):
        shift_ref[k - 1] = ext_ref[k:k + shift_ref.shape[1], :]

    def chunk(r, carry):
        base = pl.multiple_of(r * CONV_ROWS, CONV_ROWS)
        acc = jnp.broadcast_to(cb_ref[...], (CONV_ROWS, CONV_CH))
        for j in range(CONV_WIDTH):
            whole, k = divmod(lead + j, SUBLANES)
            rows = pl.ds(base + whole * SUBLANES, CONV_ROWS)
            window = ext_ref[rows, :] if k == 0 else shift_ref[k - 1, rows, :]
            acc = acc + window * cw_ref[j:j + 1, :]
        y = _layer_norm_swish(acc, lg_ref[...], lb_ref[...])
        mix_ref[pl.ds(base, CONV_ROWS), :] = y.astype(BF16)
        return carry

    lax.fori_loop(0, tt // CONV_ROWS, chunk, 0)
    mo = _mem_attn(qm_ref[...], mk_ref[...].astype(BF16), mv_ref[...].astype(BF16))
    o_ref[...] = _out_proj_residual(x_ref[...], mix_ref[...], mo, wo_ref, gp_ref)


def _mix_given_kernel(x_ref, mix_ref, qm_ref, mk_ref, mv_ref, wo_ref, gp_ref, o_ref):
    mo = _mem_attn(qm_ref[...], mk_ref[...].astype(BF16), mv_ref[...].astype(BF16))
    o_ref[...] = _out_proj_residual(x_ref[...], mix_ref[...], mo, wo_ref, gp_ref)


def _mix_prompt(x, main, qm, mk_all, mv_all, layer, wo_bf16, g_post, conv_params=None):
    n = x.shape[0]
    tt = ROW_TILE
    tps = SEQ // tt
    row = lambda i: (i, 0)
    const = lambda i: (0, 0)
    mem_idx = lambda i: (layer * BATCH + i // tps, 0)
    x_spec = pl.BlockSpec((tt, D_MODEL), row)
    main_spec = pl.BlockSpec((tt, CONV_CH), row)
    qm_spec = pl.BlockSpec((tt, MEM_WIDTH), row)
    mem_spec = pl.BlockSpec((N_MEM, MEM_WIDTH), mem_idx)
    wo_spec = pl.BlockSpec((D_MODEL, D_MODEL), const)
    gp_spec = pl.BlockSpec((1, D_MODEL), const)
    ch_spec = pl.BlockSpec((1, CONV_CH), const)
    if conv_params is not None:
        cw, cb, lg, lb = conv_params
        halo_blocks = tt // CONV_HALO
        halo_spec = pl.BlockSpec((CONV_HALO, CONV_CH),
                                 lambda i: (jnp.maximum(i * halo_blocks - 1, 0), 0))
        return pl.pallas_call(
            functools.partial(_mix_conv_kernel, tt=tt, tiles_per_seq=tps),
            grid=(n // tt,),
            in_specs=[x_spec, main_spec, halo_spec, qm_spec, mem_spec, mem_spec,
                      pl.BlockSpec((CONV_WIDTH, CONV_CH), const), ch_spec, ch_spec, ch_spec,
                      wo_spec, gp_spec],
            out_specs=x_spec,
            out_shape=jax.ShapeDtypeStruct((n, D_MODEL), F32),
            scratch_shapes=[pltpu.VMEM((tt + CONV_HALO, CONV_CH), F32),
                            pltpu.VMEM((SUBLANES - 1, tt + CONV_HALO - SUBLANES, CONV_CH), F32),
                            pltpu.VMEM((tt, CONV_CH), BF16)],
            compiler_params=_params(48, ("arbitrary",)),
            name="mix_conv",
        )(x, main, main, qm, mk_all, mv_all, cw, cb.reshape(1, CONV_CH), lg.reshape(1, CONV_CH),
          lb.reshape(1, CONV_CH), wo_bf16, g_post.reshape(1, D_MODEL))
    return pl.pallas_call(
        _mix_given_kernel,
        grid=(n // tt,),
        in_specs=[x_spec, main_spec, qm_spec, mem_spec, mem_spec, wo_spec, gp_spec],
        out_specs=x_spec,
        out_shape=jax.ShapeDtypeStruct((n, D_MODEL), F32),
        compiler_params=_params(40, ("arbitrary",)),
        name="mix_given",
    )(x, main, qm, mk_all, mv_all, wo_bf16, g_post.reshape(1, D_MODEL))


def _sample_mem_attn(qm_ref, mk_ref, mv_ref, mo_ref):
    def body(b, carry):
        q = jnp.broadcast_to(qm_ref[pl.ds(b, 1), :], (SUBLANES, MEM_WIDTH)).astype(BF16)
        o = _mem_attn(q, mk_ref[b].astype(BF16), mv_ref[b].astype(BF16))
        mo_ref[pl.ds(b, 1), :] = o[0:1, :]
        return carry

    lax.fori_loop(0, DEC_BATCH, body, 0)


def _mix_sample_conv_kernel(x_ref, u_ref, st_ref, qm_ref, mk_ref, mv_ref, cw_ref, cb_ref, lg_ref,
                            lb_ref, wo_ref, gp_ref, o_ref, nst_ref, y_ref, mo_ref):
    hist = CONV_WIDTH - 1
    w_hist = cw_ref[0:hist, :]
    for b in range(DEC_BATCH):
        st = st_ref[b]
        u_row = u_ref[b:b + 1, :]
        y_ref[b:b + 1, :] = (jnp.sum(st * w_hist, axis=0, keepdims=True)
                             + u_row * cw_ref[hist:hist + 1, :])
        nst_ref[b, 0:hist - 1, :] = st[1:hist, :]
        nst_ref[b, hist - 1:hist, :] = u_row
    mix = _layer_norm_swish(y_ref[...] + cb_ref[...], lg_ref[...], lb_ref[...])
    _sample_mem_attn(qm_ref, mk_ref, mv_ref, mo_ref)
    o_ref[...] = _out_proj_residual(x_ref[...], mix.astype(BF16), mo_ref[...], wo_ref, gp_ref)


def _mix_sample_given_kernel(x_ref, mix_ref, qm_ref, mk_ref, mv_ref, wo_ref, gp_ref, o_ref, mo_ref):
    _sample_mem_attn(qm_ref, mk_ref, mv_ref, mo_ref)
    o_ref[...] = _out_proj_residual(x_ref[...], mix_ref[...].astype(BF16), mo_ref[...], wo_ref,
                                    gp_ref)


def _full(shape):
    return pl.BlockSpec(shape, lambda i: (0,) * len(shape))


def _mix_sample(x, main, qm, mk, mv, wo_bf16, g_post, conv_params=None, state=None):
    nb = DEC_BATCH
    common_in = [_full((nb, MEM_WIDTH)), _full((nb, N_MEM, MEM_WIDTH)),
                 _full((nb, N_MEM, MEM_WIDTH))]
    tail_in = [_full((D_MODEL, D_MODEL)), _full((1, D_MODEL))]
    x_spec = _full((nb, D_MODEL))
    main_spec = _full((nb, CONV_CH))
    mo_scratch = pltpu.VMEM((nb, MEM_WIDTH), F32)
    if conv_params is not None:
        cw, cb, lg, lb = conv_params
        hist = CONV_WIDTH - 1
        ch = _full((1, CONV_CH))
        return pl.pallas_call(
            _mix_sample_conv_kernel,
            grid=(1,),
            in_specs=[x_spec, main_spec, _full((nb, hist, CONV_CH))] + common_in
                     + [_full((CONV_WIDTH, CONV_CH)), ch, ch, ch] + tail_in,
            out_specs=[x_spec, _full((nb, hist, CONV_CH))],
            out_shape=[jax.ShapeDtypeStruct((nb, D_MODEL), F32),
                       jax.ShapeDtypeStruct((nb, hist, CONV_CH), F32)],
            scratch_shapes=[pltpu.VMEM((nb, CONV_CH), F32), mo_scratch],
            compiler_params=_params(56, ("arbitrary",)),
            name="mix_sample_conv",
        )(x, main, state, qm, mk, mv, cw, cb.reshape(1, CONV_CH), lg.reshape(1, CONV_CH),
          lb.reshape(1, CONV_CH), wo_bf16, g_post.reshape(1, D_MODEL))
    return pl.pallas_call(
        _mix_sample_given_kernel,
        grid=(1,),
        in_specs=[x_spec, main_spec] + common_in + tail_in,
        out_specs=x_spec,
        out_shape=jax.ShapeDtypeStruct((nb, D_MODEL), F32),
        scratch_shapes=[mo_scratch],
        compiler_params=_params(56, ("arbitrary",)),
        name="mix_sample_given",
    )(x, main, qm, mk, mv, wo_bf16, g_post.reshape(1, D_MODEL))


def _ffn_kernel(x_ref, g1_ref, wup_ref, wdn_ref, g2_ref, o_ref):
    x = x_ref[...]
    h = _rms(x, g1_ref[...]).astype(BF16)
    g = jnp.dot(h, wup_ref[:, :D_FF], preferred_element_type=F32)
    u = jnp.dot(h, wup_ref[:, D_FF:], preferred_element_type=F32)
    a = (g * jax.nn.sigmoid(g) * u).astype(BF16)
    y = jnp.dot(a, wdn_ref[...], preferred_element_type=F32)
    o_ref[...] = x + _rms(y, g2_ref[...])


def _ffn(x, g_pre, wup_bf16, wdn_bf16, g_post):
    n = x.shape[0]
    tm = min(ROW_TILE, n)
    const = lambda i: (0, 0)
    single = pl.Buffered(1)
    return pl.pallas_call(
        _ffn_kernel,
        grid=(n // tm,),
        in_specs=[
            pl.BlockSpec((tm, D_MODEL), lambda i: (i, 0)),
            pl.BlockSpec((1, D_MODEL), const),
            pl.BlockSpec((D_MODEL, 2 * D_FF), const, pipeline_mode=single),
            pl.BlockSpec((D_FF, D_MODEL), const, pipeline_mode=single),
            pl.BlockSpec((1, D_MODEL), const),
        ],
        out_specs=pl.BlockSpec((tm, D_MODEL), lambda i: (i, 0)),
        out_shape=jax.ShapeDtypeStruct((n, D_MODEL), F32),
        compiler_params=_params(56, ("arbitrary",)),
        name="ffn",
    )(x, g_pre.reshape(1, D_MODEL), wup_bf16, wdn_bf16, g_post.reshape(1, D_MODEL))


def _kv_kernel(x_ref, g_ref, w_ref, k_ref, v_ref, kb_ref, vb_ref):
    h = _rms(x_ref[...], g_ref[...]).astype(BF16)
    kv = jnp.dot(h, w_ref[...], preferred_element_type=F32)
    k = kv[:, :SB_WIDTH]
    v = kv[:, SB_WIDTH:]
    k_ref[...] = k
    v_ref[...] = v
    kb_ref[...] = k.astype(BF16)
    vb_ref[...] = v.astype(BF16)


def _kv(x, g, w_bf16):
    n = x.shape[0]
    tm = min(ROW_TILE, n)
    const = lambda i: (0, 0)
    row = lambda i: (i, 0)
    o32 = jax.ShapeDtypeStruct((n, SB_WIDTH), F32)
    o16 = jax.ShapeDtypeStruct((n, SB_WIDTH), BF16)
    blk = pl.BlockSpec((tm, SB_WIDTH), row)
    return pl.pallas_call(
        _kv_kernel,
        grid=(n // tm,),
        in_specs=[pl.BlockSpec((tm, D_MODEL), row), pl.BlockSpec((1, D_MODEL), const),
                  pl.BlockSpec((D_MODEL, 2 * SB_WIDTH), const)],
        out_specs=[blk, blk, blk, blk],
        out_shape=[o32, o32, o16, o16],
        compiler_params=_params(40, ("arbitrary",)),
        name="kv_proj",
    )(x, g.reshape(1, D_MODEL), w_bf16)


def _sb_tile_order(nq):
    return [(qi, t) for qi in range(1, nq) for t in range(qi - 1, -1, -1)]


def _tile_rows(t, size):
    start = t * size
    if not isinstance(start, int):
        start = pl.multiple_of(start, size)
    return pl.ds(start, size)


def _sb_prompt_kernel(qi_tbl, kt_tbl, first_tbl, bias_ref, q_ref, k_ref, v_ref, o_ref,
                      q2_ref, k2_ref, ve_ref, vo_ref, later_ref, a0_ref, a1_ref, l0_ref, l1_ref,
                      w0_ref, w1_ref, cum_ref, acc_ref, *, tq, tk):
    nq = SEQ // tq
    pair = pl.program_id(1)
    a_refs, l_refs, w_refs = (a0_ref, a1_ref), (l0_ref, l1_ref), (w0_ref, w1_ref)

    lane = lax.broadcasted_iota(jnp.int32, (tq, LANES), 1)
    low_q = lane < SB_HEAD_DIM
    zero_q = jnp.zeros((tq, LANES), BF16)
    for e in range(2):
        extra = jnp.zeros((tq, LANES), F32)
        for part in range(SB_BIAS_PARTS):
            value = bias_ref[(2 * pair + e) * SB_BIAS_PARTS + part]
            extra = jnp.where(lane == part, value, extra)
        for r in range(nq):
            q2_ref[r, e * tq:(e + 1) * tq, LANES:] = extra.astype(BF16)
    for r in range(nq):
        q_r = q_ref[r * tq:(r + 1) * tq, :]
        q2_ref[r, 0:tq, 0:LANES] = jnp.where(low_q, q_r, zero_q)
        q2_ref[r, tq:, 0:LANES] = jnp.where(low_q, zero_q, q_r)
    k2_ref[:, 0:LANES] = k_ref[...]
    lane_k = lax.broadcasted_iota(jnp.int32, (SEQ, LANES), 1)
    k2_ref[:, LANES:] = jnp.where(lane_k < SB_BIAS_PARTS, 1.0, 0.0).astype(BF16)
    v_all = v_ref[...]
    low_v = lax.broadcasted_iota(jnp.int32, v_all.shape, 1) < SB_HEAD_DIM
    ve_ref[...] = jnp.where(low_v, v_all, jnp.zeros_like(v_all))
    vo_ref[...] = jnp.where(low_v, jnp.zeros_like(v_all), v_all)
    jj = lax.broadcasted_iota(jnp.int32, (tk, tk), 0)
    ss = lax.broadcasted_iota(jnp.int32, (tk, tk), 1)
    later_ref[...] = jnp.where(jj > ss, 1.0, 0.0).astype(BF16)

    def stage1(qi, t, slot, cum, diagonal):
        z = lax.dot_general(q2_ref[qi], k2_ref[_tile_rows(t, tk), :], (((1,), (1,)), ((), ())),
                            preferred_element_type=F32)
        sp = jnp.maximum(z, 0.0) + jnp.log(1.0 + jnp.exp2(jnp.abs(z) * (-LOG2E)))
        shifted = (z - sp) - cum
        if diagonal:
            rr = lax.broadcasted_iota(jnp.int32, (2 * tq, tk), 0) % tq
            cc = lax.broadcasted_iota(jnp.int32, (2 * tq, tk), 1)
            causal = cc < rr
            sp = jnp.where(causal, sp, 0.0)
            shifted = jnp.where(causal, shifted, MASKED_LOG)
        a_refs[slot][...] = sp.astype(BF16)
        l_refs[slot][...] = shifted
        return jnp.sum(sp, axis=-1, keepdims=True)

    def stage2(slot):
        after = jnp.dot(a_refs[slot][...], later_ref[...], preferred_element_type=F32)
        w_refs[slot][...] = jnp.exp(l_refs[slot][...] - after).astype(BF16)

    def stage3(t, slot):
        rows = _tile_rows(t, tk)
        pv = jnp.dot(w_refs[slot][0:tq, :], ve_ref[rows, :], preferred_element_type=F32)
        return pv + jnp.dot(w_refs[slot][tq:, :], vo_ref[rows, :], preferred_element_type=F32)

    zero_cum = jnp.zeros((2 * tq, 1), F32)
    for i in range(nq + 2):
        if i < nq:
            cum_ref[i] = stage1(i, i, i % 2, zero_cum, True)
        if 1 <= i <= nq:
            stage2((i - 1) % 2)
        if i >= 2:
            d = i - 2
            pv = stage3(d, d % 2)
            acc_ref[d] = pv
            if d == 0:
                o_ref[0:tq, :] = pv.astype(o_ref.dtype)

    n = len(_sb_tile_order(nq))

    def s1_off(i, slot):
        qi = qi_tbl[i]
        cum = cum_ref[jnp.where(first_tbl[i] != 0, qi, nq)]
        cum_ref[nq] = cum + stage1(qi, kt_tbl[i], slot, cum, False)

    def s3_off(i, slot):
        qi = qi_tbl[i]
        acc = acc_ref[jnp.where(first_tbl[i] != 0, qi, nq)] + stage3(kt_tbl[i], slot)
        acc_ref[nq] = acc
        o_ref[_tile_rows(qi, tq), :] = acc.astype(o_ref.dtype)

    s1_off(0, 0)
    s1_off(1, 1)
    stage2(0)

    def body(c, carry):
        i = 2 * c + 2
        s1_off(i, 0)
        stage2(1)
        s3_off(i - 2, 0)
        s1_off(i + 1, 1)
        stage2(0)
        s3_off(i - 1, 1)
        return carry

    lax.fori_loop(0, (n - 2) // 2, body, 0)
    stage2(1)
    s3_off(n - 2, 0)
    s3_off(n - 1, 1)


def _sb_prompt(q_bf16, k_bf16, v_bf16, bias):
    tq, tk = SB_TQ, SB_TK
    assert tq == tk
    nq = SEQ // tq
    pairs = SB_HEADS // 2
    order = _sb_tile_order(nq)
    assert len(order) % 2 == 0 and len(order) >= 2
    qi_tbl = jnp.asarray([qi for qi, _ in order], jnp.int32)
    kt_tbl = jnp.asarray([t for _, t in order], jnp.int32)
    first_tbl = jnp.asarray([int(t == qi - 1) for qi, t in order], jnp.int32)
    parts, rest = [], bias.astype(F32)
    for _ in range(SB_BIAS_PARTS):
        part = rest.astype(BF16).astype(F32)
        parts.append(part)
        rest = rest - part
    bias_parts = jnp.stack(parts, axis=-1).reshape(SB_HEADS * SB_BIAS_PARTS)
    seq_blk = pl.BlockSpec((SEQ, LANES), lambda b, p, *_: (b, p))
    grid_spec = pltpu.PrefetchScalarGridSpec(
        num_scalar_prefetch=3,
        grid=(BATCH, pairs),
        in_specs=[pl.BlockSpec(memory_space=pltpu.SMEM), seq_blk, seq_blk, seq_blk],
        out_specs=seq_blk,
        scratch_shapes=[
            pltpu.VMEM((nq, 2 * tq, 2 * LANES), BF16),
            pltpu.VMEM((SEQ, 2 * LANES), BF16),
            pltpu.VMEM((SEQ, LANES), BF16),
            pltpu.VMEM((SEQ, LANES), BF16),
            pltpu.VMEM((tk, tk), BF16),
            pltpu.VMEM((2 * tq, tk), BF16),
            pltpu.VMEM((2 * tq, tk), BF16),
            pltpu.VMEM((2 * tq, tk), F32),
            pltpu.VMEM((2 * tq, tk), F32),
            pltpu.VMEM((2 * tq, tk), BF16),
            pltpu.VMEM((2 * tq, tk), BF16),
            pltpu.VMEM((nq + 1, 2 * tq, 1), F32),
            pltpu.VMEM((nq + 1, tq, LANES), F32),
        ],
    )
    return pl.pallas_call(
        functools.partial(_sb_prompt_kernel, tq=tq, tk=tk),
        grid_spec=grid_spec,
        out_shape=jax.ShapeDtypeStruct((N_PROMPT, SB_WIDTH), BF16),
        compiler_params=_params(40, ("arbitrary", "arbitrary")),
        name="sb_prompt",
    )(qi_tbl, kt_tbl, first_tbl, bias_parts, q_bf16, k_bf16, v_bf16)


def _sb_sample_kernel(pt_ref, q_ref, bias_ref, *refs):
    del pt_ref
    npg = PAGES_PER_STEP
    hp = SB_ROWS_PER_PAGE
    k_refs = refs[:npg]
    v_refs = refs[npg:2 * npg]
    o_ref = refs[2 * npg]
    qb_ref, z_ref, w_ref, acc_ref, carry_ref = refs[2 * npg + 1:]
    j = pl.program_id(1)
    nrow = npg * hp

    @pl.when(j == 0)
    def _():
        qb_ref[...] = jnp.broadcast_to(q_ref[0], (PAGE_SIZE, SB_WIDTH)).T
        acc_ref[...] = jnp.zeros_like(acc_ref)
        carry_ref[...] = jnp.zeros_like(carry_ref)
        z_ref[...] = jnp.zeros_like(z_ref)

    for p in range(npg):
        for h in range(SB_HEADS):
            ch = slice(h * SB_HEAD_DIM, (h + 1) * SB_HEAD_DIM)
            prod = k_refs[p][0, ch, :] * qb_ref[ch, :]
            z_ref[p * hp + h:p * hp + h + 1, :] = jnp.sum(prod, axis=0, keepdims=True)
    z = z_ref[...] + bias_ref[...]
    sp = _softplus(z)
    log_beta = z - sp
    hi, lo = _split_bf16(sp)
    jj = lax.broadcasted_iota(jnp.int32, (PAGE_SIZE, PAGE_SIZE), 0)
    ss = lax.broadcasted_iota(jnp.int32, (PAGE_SIZE, PAGE_SIZE), 1)
    later = jnp.where(jj > ss, 1.0, 0.0).astype(BF16)
    ones = jnp.ones((PAGE_SIZE, LANES), BF16)
    after = (jnp.dot(hi, later, preferred_element_type=F32)
             + jnp.dot(lo, later, preferred_element_type=F32))
    total = (jnp.dot(hi, ones, preferred_element_type=F32)
             + jnp.dot(lo, ones, preferred_element_type=F32))
    t_hi, t_lo = _split_bf16(total)
    rr = lax.broadcasted_iota(jnp.int32, (2 * nrow, nrow), 0)
    cc = lax.broadcasted_iota(jnp.int32, (2 * nrow, nrow), 1)
    same_head = (rr % hp) == (cc % hp)
    earlier = (cc // hp) < ((rr % nrow) // hp)
    pick = jnp.where(same_head & (earlier | (rr >= nrow)), 1.0, 0.0).astype(BF16)
    sums = (jnp.dot(pick, t_hi, preferred_element_type=F32)
            + jnp.dot(pick, t_lo, preferred_element_type=F32))
    carry = carry_ref[...]
    w_ref[...] = jnp.exp(log_beta - after - sums[0:nrow] - carry)
    carry_ref[...] = carry + sums[nrow:]

    for h in range(SB_HEADS):
        ch = slice(h * SB_HEAD_DIM, (h + 1) * SB_HEAD_DIM)
        acc = acc_ref[ch, :]
        for p in range(npg):
            acc = acc + v_refs[p][0, ch, :] * w_ref[p * hp + h:p * hp + h + 1, :]
        acc_ref[ch, :] = acc

    @pl.when(j == pl.num_programs(1) - 1)
    def _():
        o_ref[0] = jnp.sum(acc_ref[...].T, axis=0, keepdims=True)


def _sb_sample(q, bias, cache_kt, cache_vt, page_table):
    npg = PAGES_PER_STEP
    hp = SB_ROWS_PER_PAGE
    assert npg * hp == PAGE_SIZE == LANES
    steps = N_PAGES // npg
    head_bias = jnp.zeros((hp,), F32).at[:SB_HEADS].set(bias)
    bias_rows = jnp.broadcast_to(jnp.tile(head_bias, npg)[:, None], (npg * hp, PAGE_SIZE))

    def page_spec(p):
        return pl.BlockSpec(
            (1, SB_WIDTH, PAGE_SIZE),
            lambda b, j, pt: (pt[b, N_PAGES - 1 - (j * npg + p)], 0, 0))

    grid_spec = pltpu.PrefetchScalarGridSpec(
        num_scalar_prefetch=1,
        grid=(DEC_BATCH, steps),
        in_specs=[pl.BlockSpec((1, 1, SB_WIDTH), lambda b, j, pt: (b, 0, 0)),
                  pl.BlockSpec((npg * hp, PAGE_SIZE), lambda b, j, pt: (0, 0))]
                 + [page_spec(p) for p in range(npg)] + [page_spec(p) for p in range(npg)],
        out_specs=pl.BlockSpec((1, 1, SB_WIDTH), lambda b, j, pt: (b, 0, 0)),
        scratch_shapes=[pltpu.VMEM((SB_WIDTH, PAGE_SIZE), F32),
                        pltpu.VMEM((npg * hp, PAGE_SIZE), F32),
                        pltpu.VMEM((npg * hp, PAGE_SIZE), F32),
                        pltpu.VMEM((SB_WIDTH, PAGE_SIZE), F32),
                        pltpu.VMEM((npg * hp, PAGE_SIZE), F32)],
    )
    out = pl.pallas_call(
        _sb_sample_kernel,
        grid_spec=grid_spec,
        out_shape=jax.ShapeDtypeStruct((DEC_BATCH, 1, SB_WIDTH), F32),
        compiler_params=_params(48, ("arbitrary", "arbitrary")),
        name="sb_sample",
    )(page_table, q.reshape(DEC_BATCH, 1, SB_WIDTH), bias_rows,
      *([cache_kt] * npg), *([cache_vt] * npg))
    return out.reshape(DEC_BATCH, SB_WIDTH)


def kernel(x_prompt, x_sample, cache_k, cache_v, state_conv, cache_mem_k, cache_mem_v, page_table, mem_prompt, norm_mix_pre, norm_mix_post, norm_ffn_pre, norm_ffn_post, w_in_a, conv_w, conv_b, conv_ln_g, conv_ln_b, w_in_b, sb_bias, kv_norm_g, w_kv, mem_norm_g, w_mem_kv, w_out, w_ffn_up, w_ffn_down):
    hist = CONV_WIDTH - 1
    mk_p, mv_p = _memkv(mem_prompt.reshape(BATCH * N_MEM, D_MODEL), mem_norm_g, w_mem_kv)
    mk_flat = mk_p.reshape(DEPTH * BATCH * N_MEM, MEM_WIDTH)
    mv_flat = mv_p.reshape(DEPTH * BATCH * N_MEM, MEM_WIDTH)
    cache_k3 = jnp.transpose(cache_k, (0, 2, 3, 1)).reshape(cache_k.shape[0], SB_WIDTH, PAGE_SIZE)
    cache_v3 = jnp.transpose(cache_v, (0, 2, 3, 1)).reshape(cache_v.shape[0], SB_WIDTH, PAGE_SIZE)

    xp = x_prompt.reshape(N_PROMPT, D_MODEL)
    xs = x_sample.reshape(DEC_BATCH, D_MODEL)
    conv_p, conv_s = [], []
    k_p = v_p = kb_p = vb_p = k_s = v_s = None
    for l in range(DEPTH):
        wo = w_out[l].astype(BF16)
        mk_s = cache_mem_k[l].reshape(DEC_BATCH, N_MEM, MEM_WIDTH)
        mv_s = cache_mem_v[l].reshape(DEC_BATCH, N_MEM, MEM_WIDTH)
        if l < N_A:
            w_in = w_in_a[l].astype(BF16)
            conv_params = (conv_w[l], conv_b[l], conv_ln_g[l], conv_ln_b[l])
            u_p, qm_p = _inproj(xp, norm_mix_pre[l], w_in, glu=True, main_dtype=F32, qm_dtype=BF16)
            u_s, qm_s = _inproj(xs, norm_mix_pre[l], w_in, glu=True, main_dtype=F32, qm_dtype=F32)
            xp = _mix_prompt(xp, u_p, qm_p, mk_flat, mv_flat, l, wo, norm_mix_post[l], conv_params)
            xs, st = _mix_sample(xs, u_s, qm_s, mk_s, mv_s, wo, norm_mix_post[l], conv_params,
                                 state_conv[l])
            conv_p.append(u_p.reshape(BATCH, SEQ, CONV_CH)[:, SEQ - hist:])
            conv_s.append(st)
        else:
            w_in = w_in_b[l - N_A].astype(BF16)
            bias = sb_bias[l - N_A]
            q_p, qm_p = _inproj(xp, norm_mix_pre[l], w_in, glu=False, main_dtype=BF16, qm_dtype=BF16)
            q_s, qm_s = _inproj(xs, norm_mix_pre[l], w_in, glu=False, main_dtype=F32, qm_dtype=F32)
            mix_p = _sb_prompt(q_p, kb_p, vb_p, bias)
            mix_s = _sb_sample(q_s, bias, cache_k3, cache_v3, page_table)
            xp = _mix_prompt(xp, mix_p, qm_p, mk_flat, mv_flat, l, wo, norm_mix_post[l])
            xs = _mix_sample(xs, mix_s, qm_s, mk_s, mv_s, wo, norm_mix_post[l])
        wup = w_ffn_up[l].astype(BF16)
        wdn = w_ffn_down[l].astype(BF16)
        xp = _ffn(xp, norm_ffn_pre[l], wup, wdn, norm_ffn_post[l])
        xs = _ffn(xs, norm_ffn_pre[l], wup, wdn, norm_ffn_post[l])
        if l == N_A - 1:
            wkv = w_kv.astype(BF16)
            k_p, v_p, kb_p, vb_p = _kv(xp, kv_norm_g, wkv)
            k_s, v_s, _, _ = _kv(xs, kv_norm_g, wkv)

    mem_shape = (DEPTH, BATCH, N_MEM, MEM_HEADS, MEM_HEAD_DIM)
    return (
        xp.reshape(BATCH, SEQ, D_MODEL),
        xs.reshape(DEC_BATCH, 1, D_MODEL),
        k_p.reshape(BATCH, SEQ, SB_HEADS, SB_HEAD_DIM),
        v_p.reshape(BATCH, SEQ, SB_HEADS, SB_HEAD_DIM),
        jnp.stack(conv_p),
        mk_p.reshape(mem_shape),
        mv_p.reshape(mem_shape),
        k_s.reshape(DEC_BATCH, 1, SB_HEADS, SB_HEAD_DIM),
        v_s.reshape(DEC_BATCH, 1, SB_HEADS, SB_HEAD_DIM),
        jnp.stack(conv_s),
    )
```

```python
import functools
import math

import jax
import jax.numpy as jnp
from jax import lax
from jax.experimental import pallas as pl
from jax.experimental.pallas import tpu as pltpu

F32 = jnp.float32
BF16 = jnp.bfloat16

D_MODEL = 1024
BATCH = 8
SEQ = 2048
DEPTH = 4
DEC_BATCH = 32
PAST_LEN = 8192
PAGE_SIZE = 128
N_PAGES = PAST_LEN // PAGE_SIZE
N_A = DEPTH // 2
CONV_CH = 3 * D_MODEL // 4
CONV_WIDTH = 31
SB_HEAD_DIM = 64
SB_HEADS = CONV_CH // SB_HEAD_DIM
SB_WIDTH = SB_HEADS * SB_HEAD_DIM
MEM_HEADS = 4
MEM_HEAD_DIM = 64
MEM_WIDTH = MEM_HEADS * MEM_HEAD_DIM
N_MEM = 256
D_FF = ((8 * D_MODEL + 3 * 256 - 1) // (3 * 256)) * 256
RMS_EPS = 1e-6
LN_EPS = 1e-5
SB_SCALE = 1.0 / math.sqrt(SB_HEAD_DIM)
MEM_SCALE = 1.0 / math.sqrt(MEM_HEAD_DIM)

N_PROMPT = BATCH * SEQ

V7X_VMEM_BYTES = 64 * 1024 * 1024
LANES = 128
SUBLANES = 8

ROW_TILE = 512
CONV_HALO = 32
CONV_ROWS = 64
SB_TQ = 256
SB_TK = 256
PAGES_PER_STEP = 16
SB_ROWS_PER_PAGE = 16
MASKED_LOG = -1e30
SB_BIAS_PARTS = 3
LOG2E = 1.4426950408889634


def _params(vmem_mib, semantics):
    return pltpu.CompilerParams(
        dimension_semantics=semantics,
        vmem_limit_bytes=min(vmem_mib * 1024 * 1024, V7X_VMEM_BYTES - 8 * 1024 * 1024),
    )


def _rms(x, g):
    return x * lax.rsqrt(jnp.mean(x * x, axis=-1, keepdims=True) + RMS_EPS) * g


def _layer_norm_swish(y, ln_g, ln_b):
    mu = jnp.mean(y, axis=-1, keepdims=True)
    d = y - mu
    var = jnp.mean(d * d, axis=-1, keepdims=True)
    y = d * lax.rsqrt(var + LN_EPS) * ln_g + ln_b
    return y * jax.nn.sigmoid(y)


def _softplus(z):
    return jnp.maximum(z, 0.0) + jnp.log(1.0 + jnp.exp(-jnp.abs(z)))


def _split_bf16(x):
    hi = x.astype(BF16)
    lo = (x - hi.astype(F32)).astype(BF16)
    return hi, lo


def _softmax_rows(s):
    s = s - jnp.max(s, axis=-1, keepdims=True)
    p = jnp.exp(s)
    return p / jnp.sum(p, axis=-1, keepdims=True)


def _mem_attn(qm, mkt, mvt, s_ref, p_ref, between=None):
    head_of_row = lax.broadcasted_iota(jnp.int32, mkt.shape, 0) // MEM_HEAD_DIM
    zero = jnp.zeros_like(mkt)
    for h in range(MEM_HEADS):
        kh = jnp.where(head_of_row == h, mkt, zero)
        s_ref[h] = jnp.dot(qm, kh, preferred_element_type=F32)
    if between is not None:
        between()
    for h in range(MEM_HEADS):
        p_ref[h] = _softmax_rows(s_ref[h]).astype(BF16)
    out = None
    for h in range(MEM_HEADS):
        vh = jnp.where(head_of_row == h, mvt, zero)
        o = lax.dot_general(p_ref[h], vh, (((1,), (1,)), ((), ())), preferred_element_type=F32)
        out = o if out is None else out + o
    return out


def _mem_attn_single(q_row, mkt, mvt):
    head_of_lane = lax.broadcasted_iota(jnp.int32, (SUBLANES, MEM_WIDTH), 1) // MEM_HEAD_DIM
    row = lax.broadcasted_iota(jnp.int32, (SUBLANES, MEM_WIDTH), 0)
    mine = head_of_lane == row
    q_heads = jnp.where(mine, jnp.broadcast_to(q_row, (SUBLANES, MEM_WIDTH)), 0.0).astype(BF16)
    p = _softmax_rows(jnp.dot(q_heads, mkt, preferred_element_type=F32))
    o = lax.dot_general(p.astype(BF16), mvt, (((1,), (1,)), ((), ())),
                        preferred_element_type=F32)
    return jnp.sum(jnp.where(mine, o, 0.0), axis=0, keepdims=True)


def _memkv_kernel(m_ref, g_ref, wt_ref, k_ref, v_ref):
    h = _rms(m_ref[...], g_ref[0]).astype(BF16)
    kvt = lax.dot_general(wt_ref[0].astype(BF16), h, (((1,), (1,)), ((), ())),
                          preferred_element_type=F32)
    k_ref[0] = kvt[:MEM_WIDTH]
    v_ref[0] = kvt[MEM_WIDTH:]


def _memkv(mem_flat, mem_norm_g, w_mem_kv_t):
    out = jax.ShapeDtypeStruct((DEPTH * BATCH, MEM_WIDTH, N_MEM), F32)
    blk = pl.BlockSpec((1, MEM_WIDTH, N_MEM), lambda l, b: (l * BATCH + b, 0, 0))
    return pl.pallas_call(
        _memkv_kernel,
        grid=(DEPTH, BATCH),
        in_specs=[
            pl.BlockSpec((N_MEM, D_MODEL), lambda l, b: (b, 0)),
            pl.BlockSpec((1, 1, D_MODEL), lambda l, b: (l, 0, 0)),
            pl.BlockSpec((1, 2 * MEM_WIDTH, D_MODEL), lambda l, b: (l, 0, 0)),
        ],
        out_specs=[blk, blk],
        out_shape=[out, out],
        compiler_params=_params(32, ("arbitrary", "arbitrary")),
        name="memkv",
    )(mem_flat, mem_norm_g.reshape(DEPTH, 1, D_MODEL), w_mem_kv_t)


def _inproj_kernel(x_ref, g_ref, w_ref, main_ref, qm_ref, *, glu):
    h = _rms(x_ref[...], g_ref[...]).astype(BF16)
    proj = jnp.dot(h, w_ref[...], preferred_element_type=F32)
    if glu:
        a = proj[:, :CONV_CH]
        gate = proj[:, CONV_CH:2 * CONV_CH]
        main_ref[...] = (a * jax.nn.sigmoid(gate)).astype(main_ref.dtype)
        qm = proj[:, 2 * CONV_CH:]
    else:
        main_ref[...] = (proj[:, :SB_WIDTH] * SB_SCALE).astype(main_ref.dtype)
        qm = proj[:, SB_WIDTH:]
    qm_ref[...] = (qm * MEM_SCALE).astype(qm_ref.dtype)


def _inproj(x, g, w_bf16, *, glu, main_dtype, qm_dtype):
    n = x.shape[0]
    tm = min(ROW_TILE, n)
    width = w_bf16.shape[1]
    main_w = CONV_CH if glu else SB_WIDTH
    return pl.pallas_call(
        functools.partial(_inproj_kernel, glu=glu),
        grid=(n // tm,),
        in_specs=[
            pl.BlockSpec((tm, D_MODEL), lambda i: (i, 0)),
            pl.BlockSpec((1, D_MODEL), lambda i: (0, 0)),
            pl.BlockSpec((D_MODEL, width), lambda i: (0, 0)),
        ],
        out_specs=[
            pl.BlockSpec((tm, main_w), lambda i: (i, 0)),
            pl.BlockSpec((tm, MEM_WIDTH), lambda i: (i, 0)),
        ],
        out_shape=[
            jax.ShapeDtypeStruct((n, main_w), main_dtype),
            jax.ShapeDtypeStruct((n, MEM_WIDTH), qm_dtype),
        ],
        compiler_params=_params(40, ("arbitrary",)),
        name="inproj_a" if glu else "inproj_b",
    )(x, g.reshape(1, D_MODEL), w_bf16)


def _out_proj_residual(x, mix_bf16, mo, wo_ref, gp_ref):
    y = jnp.dot(mix_bf16, wo_ref[0:CONV_CH, :], preferred_element_type=F32)
    y = y + jnp.dot(mo.astype(BF16), wo_ref[CONV_CH:, :], preferred_element_type=F32)
    return x + _rms(y, gp_ref[...])


def _mem_attn_out_proj(x_ref, mix_ref, qm_ref, mk_ref, mv_ref, wo_ref, gp_ref, o_ref, s_ref, p_ref,
                       y_ref):
    def mix_part():
        y_ref[...] = jnp.dot(mix_ref[...], wo_ref[0:CONV_CH, :], preferred_element_type=F32)

    mo = _mem_attn(qm_ref[...], mk_ref[0].astype(BF16), mv_ref[0].astype(BF16), s_ref, p_ref,
                   mix_part)
    y = y_ref[...] + jnp.dot(mo.astype(BF16), wo_ref[CONV_CH:, :], preferred_element_type=F32)
    o_ref[...] = x_ref[...] + _rms(y, gp_ref[...])


def _mix_conv_kernel(x_ref, u_ref, halo_ref, qm_ref, mk_ref, mv_ref, cw_ref, cb_ref, lg_ref,
                     lb_ref, wo_ref, gp_ref, o_ref, ext_ref, shift_ref, conv_ref, mix_ref, s_ref,
                     p_ref, y_ref, *, tt, tiles_per_seq):
    first = (pl.program_id(0) % tiles_per_seq) == 0
    ext_ref[0:CONV_HALO, :] = jnp.where(first, 0.0, halo_ref[...])
    ext_ref[CONV_HALO:, :] = u_ref[...]
    lead = CONV_HALO - (CONV_WIDTH - 1)
    for k in range(1, SUBLANES):
        shift_ref[k - 1] = ext_ref[k:k + shift_ref.shape[1], :]

    groups = CONV_ROWS // SUBLANES
    for lt in range(CONV_CH // LANES):
        lanes = slice(lt * LANES, (lt + 1) * LANES)
        taps = [jnp.broadcast_to(cw_ref[j:j + 1, lanes], (SUBLANES, LANES))
                for j in range(CONV_WIDTH)]
        bias = jnp.broadcast_to(cb_ref[:, lanes], (SUBLANES, LANES))

        def conv_chunk(r, carry, lanes=lanes, taps=taps, bias=bias):
            base = pl.multiple_of(r * CONV_ROWS, CONV_ROWS)
            acc = [bias] * groups
            for k in range(SUBLANES):
                wholes = [o // SUBLANES for o in range(lead, lead + CONV_WIDTH)
                          if o % SUBLANES == k]
                first, last = wholes[0], wholes[-1]
                rows = pl.ds(base + first * SUBLANES, CONV_ROWS + (last - first) * SUBLANES)
                window = ext_ref[rows, lanes] if k == 0 else shift_ref[k - 1, rows, lanes]
                for whole in wholes:
                    tap = taps[whole * SUBLANES + k - lead]
                    for g in range(groups):
                        off = (whole - first + g) * SUBLANES
                        acc[g] = acc[g] + window[off:off + SUBLANES] * tap
            for g in range(groups):
                conv_ref[pl.ds(base + g * SUBLANES, SUBLANES), lanes] = acc[g]
            return carry

        lax.fori_loop(0, tt // CONV_ROWS, conv_chunk, 0)

    mix_ref[...] = _layer_norm_swish(conv_ref[...], lg_ref[...], lb_ref[...]).astype(BF16)
    _mem_attn_out_proj(x_ref, mix_ref, qm_ref, mk_ref, mv_ref, wo_ref, gp_ref, o_ref, s_ref, p_ref,
                       y_ref)


def _mix_given_kernel(x_ref, mix_ref, qm_ref, mk_ref, mv_ref, wo_ref, gp_ref, o_ref, s_ref, p_ref,
                      y_ref):
    _mem_attn_out_proj(x_ref, mix_ref, qm_ref, mk_ref, mv_ref, wo_ref, gp_ref, o_ref, s_ref, p_ref,
                       y_ref)


def _mix_prompt(x, main, qm, mk_all, mv_all, layer, wo_bf16, g_post, conv_params=None):
    n = x.shape[0]
    tt = ROW_TILE
    tps = SEQ // tt
    row = lambda i: (i, 0)
    const = lambda i: (0, 0)
    mem_idx = lambda i: (layer * BATCH + i // tps, 0, 0)
    x_spec = pl.BlockSpec((tt, D_MODEL), row)
    main_spec = pl.BlockSpec((tt, CONV_CH), row)
    qm_spec = pl.BlockSpec((tt, MEM_WIDTH), row)
    mem_spec = pl.BlockSpec((1, MEM_WIDTH, N_MEM), mem_idx)
    attn_scratch = [pltpu.VMEM((MEM_HEADS, tt, N_MEM), F32),
                    pltpu.VMEM((MEM_HEADS, tt, N_MEM), BF16),
                    pltpu.VMEM((tt, D_MODEL), F32)]
    wo_spec = pl.BlockSpec((D_MODEL, D_MODEL), const)
    gp_spec = pl.BlockSpec((1, D_MODEL), const)
    ch_spec = pl.BlockSpec((1, CONV_CH), const)
    if conv_params is not None:
        cw, cb, lg, lb = conv_params
        halo_blocks = tt // CONV_HALO
        halo_spec = pl.BlockSpec((CONV_HALO, CONV_CH),
                                 lambda i: (jnp.maximum(i * halo_blocks - 1, 0), 0))
        return pl.pallas_call(
            functools.partial(_mix_conv_kernel, tt=tt, tiles_per_seq=tps),
            grid=(n // tt,),
            in_specs=[x_spec, main_spec, halo_spec, qm_spec, mem_spec, mem_spec,
                      pl.BlockSpec((CONV_WIDTH, CONV_CH), const), ch_spec, ch_spec, ch_spec,
                      wo_spec, gp_spec],
            out_specs=x_spec,
            out_shape=jax.ShapeDtypeStruct((n, D_MODEL), F32),
            scratch_shapes=[pltpu.VMEM((tt + CONV_HALO, CONV_CH), F32),
                            pltpu.VMEM((SUBLANES - 1, tt + CONV_HALO - SUBLANES, CONV_CH), F32),
                            pltpu.VMEM((tt, CONV_CH), F32),
                            pltpu.VMEM((tt, CONV_CH), BF16)] + attn_scratch,
            compiler_params=_params(52, ("arbitrary",)),
            name="mix_conv",
        )(x, main, main, qm, mk_all, mv_all, cw, cb.reshape(1, CONV_CH), lg.reshape(1, CONV_CH),
          lb.reshape(1, CONV_CH), wo_bf16, g_post.reshape(1, D_MODEL))
    return pl.pallas_call(
        _mix_given_kernel,
        grid=(n // tt,),
        in_specs=[x_spec, main_spec, qm_spec, mem_spec, mem_spec, wo_spec, gp_spec],
        out_specs=x_spec,
        out_shape=jax.ShapeDtypeStruct((n, D_MODEL), F32),
        scratch_shapes=attn_scratch,
        compiler_params=_params(40, ("arbitrary",)),
        name="mix_given",
    )(x, main, qm, mk_all, mv_all, wo_bf16, g_post.reshape(1, D_MODEL))


def _sample_mem_attn(qm_ref, mk_ref, mv_ref, mo_ref):
    def body(b, carry):
        mo_ref[pl.ds(b, 1), :] = _mem_attn_single(
            qm_ref[pl.ds(b, 1), :], mk_ref[0, b].astype(BF16), mv_ref[0, b].astype(BF16))
        return carry

    lax.fori_loop(0, DEC_BATCH, body, 0)


def _mix_sample_conv_kernel(x_ref, u_ref, st_ref, qm_ref, mk_ref, mv_ref, cw_ref, cb_ref, lg_ref,
                            lb_ref, wo_ref, gp_ref, o_ref, nst_ref, y_ref, mo_ref):
    hist = CONV_WIDTH - 1
    w_hist = cw_ref[0:hist, :]
    for b in range(DEC_BATCH):
        st = st_ref[b]
        u_row = u_ref[b:b + 1, :]
        y_ref[b:b + 1, :] = (jnp.sum(st * w_hist, axis=0, keepdims=True)
                             + u_row * cw_ref[hist:hist + 1, :])
        nst_ref[b, 0:hist - 1, :] = st[1:hist, :]
        nst_ref[b, hist - 1:hist, :] = u_row
    mix = _layer_norm_swish(y_ref[...] + cb_ref[...], lg_ref[...], lb_ref[...])
    _sample_mem_attn(qm_ref, mk_ref, mv_ref, mo_ref)
    o_ref[...] = _out_proj_residual(x_ref[...], mix.astype(BF16), mo_ref[...], wo_ref, gp_ref)


def _mix_sample_given_kernel(x_ref, mix_ref, qm_ref, mk_ref, mv_ref, wo_ref, gp_ref, o_ref, mo_ref):
    _sample_mem_attn(qm_ref, mk_ref, mv_ref, mo_ref)
    o_ref[...] = _out_proj_residual(x_ref[...], mix_ref[...].astype(BF16), mo_ref[...], wo_ref,
                                    gp_ref)


def _full(shape):
    return pl.BlockSpec(shape, lambda i: (0,) * len(shape))


def _mix_sample(x, main, qm, mk, mv, layer, wo_bf16, g_post, conv_params=None, state=None):
    nb = DEC_BATCH
    mem_spec = pl.BlockSpec((1, nb, MEM_WIDTH, N_MEM), lambda i: (layer, 0, 0, 0))
    common_in = [_full((nb, MEM_WIDTH)), mem_spec, mem_spec]
    tail_in = [_full((D_MODEL, D_MODEL)), _full((1, D_MODEL))]
    x_spec = _full((nb, D_MODEL))
    main_spec = _full((nb, CONV_CH))
    mo_scratch = pltpu.VMEM((nb, MEM_WIDTH), F32)
    if conv_params is not None:
        cw, cb, lg, lb = conv_params
        hist = CONV_WIDTH - 1
        ch = _full((1, CONV_CH))
        return pl.pallas_call(
            _mix_sample_conv_kernel,
            grid=(1,),
            in_specs=[x_spec, main_spec, _full((nb, hist, CONV_CH))] + common_in
                     + [_full((CONV_WIDTH, CONV_CH)), ch, ch, ch] + tail_in,
            out_specs=[x_spec, _full((nb, hist, CONV_CH))],
            out_shape=[jax.ShapeDtypeStruct((nb, D_MODEL), F32),
                       jax.ShapeDtypeStruct((nb, hist, CONV_CH), F32)],
            scratch_shapes=[pltpu.VMEM((nb, CONV_CH), F32), mo_scratch],
            compiler_params=_params(56, ("arbitrary",)),
            name="mix_sample_conv",
        )(x, main, state, qm, mk, mv, cw, cb.reshape(1, CONV_CH), lg.reshape(1, CONV_CH),
          lb.reshape(1, CONV_CH), wo_bf16, g_post.reshape(1, D_MODEL))
    return pl.pallas_call(
        _mix_sample_given_kernel,
        grid=(1,),
        in_specs=[x_spec, main_spec] + common_in + tail_in,
        out_specs=x_spec,
        out_shape=jax.ShapeDtypeStruct((nb, D_MODEL), F32),
        scratch_shapes=[mo_scratch],
        compiler_params=_params(56, ("arbitrary",)),
        name="mix_sample_given",
    )(x, main, qm, mk, mv, wo_bf16, g_post.reshape(1, D_MODEL))


def _ffn_kernel(x_ref, g1_ref, wup_ref, wdn_ref, g2_ref, o_ref):
    x = x_ref[...]
    h = _rms(x, g1_ref[...]).astype(BF16)
    g = jnp.dot(h, wup_ref[0, :, :D_FF], preferred_element_type=F32)
    u = jnp.dot(h, wup_ref[0, :, D_FF:], preferred_element_type=F32)
    a = (g * jax.nn.sigmoid(g) * u).astype(BF16)
    y = jnp.dot(a, wdn_ref[0], preferred_element_type=F32)
    o_ref[...] = x + _rms(y, g2_ref[...])


def _ffn(x, g_pre, wup_bf16, wdn_bf16, layer, g_post):
    n = x.shape[0]
    tm = min(ROW_TILE, n)
    const = lambda i: (0, 0)
    single = pl.Buffered(1)
    return pl.pallas_call(
        _ffn_kernel,
        grid=(n // tm,),
        in_specs=[
            pl.BlockSpec((tm, D_MODEL), lambda i: (i, 0)),
            pl.BlockSpec((1, D_MODEL), const),
            pl.BlockSpec((1, D_MODEL, 2 * D_FF), lambda i: (layer, 0, 0), pipeline_mode=single),
            pl.BlockSpec((1, D_FF, D_MODEL), lambda i: (layer, 0, 0), pipeline_mode=single),
            pl.BlockSpec((1, D_MODEL), const),
        ],
        out_specs=pl.BlockSpec((tm, D_MODEL), lambda i: (i, 0)),
        out_shape=jax.ShapeDtypeStruct((n, D_MODEL), F32),
        compiler_params=_params(56, ("arbitrary",)),
        name="ffn",
    )(x, g_pre.reshape(1, D_MODEL), wup_bf16, wdn_bf16, g_post.reshape(1, D_MODEL))


def _kv_kernel(x_ref, g_ref, w_ref, k_ref, v_ref, kb_ref, vb_ref):
    h = _rms(x_ref[...], g_ref[...]).astype(BF16)
    kv = jnp.dot(h, w_ref[...], preferred_element_type=F32)
    k = kv[:, :SB_WIDTH]
    v = kv[:, SB_WIDTH:]
    k_ref[...] = k
    v_ref[...] = v
    kb_ref[...] = k.astype(BF16)
    vb_ref[...] = v.astype(BF16)


def _kv(x, g, w_bf16):
    n = x.shape[0]
    tm = min(ROW_TILE, n)
    const = lambda i: (0, 0)
    row = lambda i: (i, 0)
    o32 = jax.ShapeDtypeStruct((n, SB_WIDTH), F32)
    o16 = jax.ShapeDtypeStruct((n, SB_WIDTH), BF16)
    blk = pl.BlockSpec((tm, SB_WIDTH), row)
    return pl.pallas_call(
        _kv_kernel,
        grid=(n // tm,),
        in_specs=[pl.BlockSpec((tm, D_MODEL), row), pl.BlockSpec((1, D_MODEL), const),
                  pl.BlockSpec((D_MODEL, 2 * SB_WIDTH), const)],
        out_specs=[blk, blk, blk, blk],
        out_shape=[o32, o32, o16, o16],
        compiler_params=_params(40, ("arbitrary",)),
        name="kv_proj",
    )(x, g.reshape(1, D_MODEL), w_bf16)


def _sb_tile_order(nq):
    return [(qi, t) for qi in range(1, nq) for t in range(qi - 1, -1, -1)]


def _tile_rows(t, size):
    start = t * size
    if not isinstance(start, int):
        start = pl.multiple_of(start, size)
    return pl.ds(start, size)


def _sb_prompt_kernel(qi_tbl, kt_tbl, first_tbl, bias_ref, q_ref, k_ref, v_ref, o_ref,
                      q2_ref, k2_ref, ve_ref, vo_ref, later_ref, a0_ref, a1_ref, l0_ref, l1_ref,
                      w0_ref, w1_ref, cum_ref, acc_ref, *, tq, tk):
    nq = SEQ // tq
    pair = pl.program_id(1)
    a_refs, l_refs, w_refs = (a0_ref, a1_ref), (l0_ref, l1_ref), (w0_ref, w1_ref)

    lane = lax.broadcasted_iota(jnp.int32, (tq, LANES), 1)
    low_q = lane < SB_HEAD_DIM
    zero_q = jnp.zeros((tq, LANES), BF16)
    for e in range(2):
        extra = jnp.zeros((tq, LANES), F32)
        for part in range(SB_BIAS_PARTS):
            value = bias_ref[(2 * pair + e) * SB_BIAS_PARTS + part]
            extra = jnp.where(lane == part, value, extra)
        for r in range(nq):
            q2_ref[r, e * tq:(e + 1) * tq, LANES:] = extra.astype(BF16)
    for r in range(nq):
        q_r = q_ref[r * tq:(r + 1) * tq, :]
        q2_ref[r, 0:tq, 0:LANES] = jnp.where(low_q, q_r, zero_q)
        q2_ref[r, tq:, 0:LANES] = jnp.where(low_q, zero_q, q_r)
    k2_ref[:, 0:LANES] = k_ref[...]
    lane_k = lax.broadcasted_iota(jnp.int32, (SEQ, LANES), 1)
    k2_ref[:, LANES:] = jnp.where(lane_k < SB_BIAS_PARTS, 1.0, 0.0).astype(BF16)
    v_all = v_ref[...]
    low_v = lax.broadcasted_iota(jnp.int32, v_all.shape, 1) < SB_HEAD_DIM
    ve_ref[...] = jnp.where(low_v, v_all, jnp.zeros_like(v_all))
    vo_ref[...] = jnp.where(low_v, jnp.zeros_like(v_all), v_all)
    jj = lax.broadcasted_iota(jnp.int32, (tk, tk), 0)
    ss = lax.broadcasted_iota(jnp.int32, (tk, tk), 1)
    later_ref[...] = jnp.where(jj > ss, 1.0, 0.0).astype(BF16)

    def stage1(qi, t, slot, cum, diagonal):
        z = lax.dot_general(q2_ref[qi], k2_ref[_tile_rows(t, tk), :], (((1,), (1,)), ((), ())),
                            preferred_element_type=F32)
        sp = jnp.maximum(z, 0.0) + jnp.log(1.0 + jnp.exp2(jnp.abs(z) * (-LOG2E)))
        shifted = (z - sp) - cum
        if diagonal:
            rr = lax.broadcasted_iota(jnp.int32, (2 * tq, tk), 0) % tq
            cc = lax.broadcasted_iota(jnp.int32, (2 * tq, tk), 1)
            causal = cc < rr
            sp = jnp.where(causal, sp, 0.0)
            shifted = jnp.where(causal, shifted, MASKED_LOG)
        a_refs[slot][...] = sp.astype(BF16)
        l_refs[slot][...] = shifted
        return jnp.sum(sp, axis=-1, keepdims=True)

    def stage2(slot):
        after = jnp.dot(a_refs[slot][...], later_ref[...], preferred_element_type=F32)
        w_refs[slot][...] = jnp.exp(l_refs[slot][...] - after).astype(BF16)

    def stage3(t, slot):
        rows = _tile_rows(t, tk)
        pv = jnp.dot(w_refs[slot][0:tq, :], ve_ref[rows, :], preferred_element_type=F32)
        return pv + jnp.dot(w_refs[slot][tq:, :], vo_ref[rows, :], preferred_element_type=F32)

    zero_cum = jnp.zeros((2 * tq, 1), F32)
    for i in range(nq + 2):
        if i < nq:
            cum_ref[i] = stage1(i, i, i % 2, zero_cum, True)
        if 1 <= i <= nq:
            stage2((i - 1) % 2)
        if i >= 2:
            d = i - 2
            pv = stage3(d, d % 2)
            acc_ref[d] = pv
            if d == 0:
                o_ref[0:tq, :] = pv.astype(o_ref.dtype)

    n = len(_sb_tile_order(nq))

    def s1_off(i, slot):
        qi = qi_tbl[i]
        cum = cum_ref[jnp.where(first_tbl[i] != 0, qi, nq)]
        cum_ref[nq] = cum + stage1(qi, kt_tbl[i], slot, cum, False)

    def s3_off(i, slot):
        qi = qi_tbl[i]
        acc = acc_ref[jnp.where(first_tbl[i] != 0, qi, nq)] + stage3(kt_tbl[i], slot)
        acc_ref[nq] = acc
        o_ref[_tile_rows(qi, tq), :] = acc.astype(o_ref.dtype)

    s1_off(0, 0)
    s1_off(1, 1)
    stage2(0)

    def body(c, carry):
        i = 2 * c + 2
        s1_off(i, 0)
        stage2(1)
        s3_off(i - 2, 0)
        s1_off(i + 1, 1)
        stage2(0)
        s3_off(i - 1, 1)
        return carry

    lax.fori_loop(0, (n - 2) // 2, body, 0)
    stage2(1)
    s3_off(n - 2, 0)
    s3_off(n - 1, 1)


def _sb_prompt(q_bf16, k_bf16, v_bf16, bias):
    tq, tk = SB_TQ, SB_TK
    assert tq == tk
    nq = SEQ // tq
    pairs = SB_HEADS // 2
    order = _sb_tile_order(nq)
    assert len(order) % 2 == 0 and len(order) >= 2
    qi_tbl = jnp.asarray([qi for qi, _ in order], jnp.int32)
    kt_tbl = jnp.asarray([t for _, t in order], jnp.int32)
    first_tbl = jnp.asarray([int(t == qi - 1) for qi, t in order], jnp.int32)
    parts, rest = [], bias.astype(F32)
    for _ in range(SB_BIAS_PARTS):
        part = rest.astype(BF16).astype(F32)
        parts.append(part)
        rest = rest - part
    bias_parts = jnp.stack(parts, axis=-1).reshape(SB_HEADS * SB_BIAS_PARTS)
    seq_blk = pl.BlockSpec((SEQ, LANES), lambda b, p, *_: (b, p))
    grid_spec = pltpu.PrefetchScalarGridSpec(
        num_scalar_prefetch=3,
        grid=(BATCH, pairs),
        in_specs=[pl.BlockSpec(memory_space=pltpu.SMEM), seq_blk, seq_blk, seq_blk],
        out_specs=seq_blk,
        scratch_shapes=[
            pltpu.VMEM((nq, 2 * tq, 2 * LANES), BF16),
            pltpu.VMEM((SEQ, 2 * LANES), BF16),
            pltpu.VMEM((SEQ, LANES), BF16),
            pltpu.VMEM((SEQ, LANES), BF16),
            pltpu.VMEM((tk, tk), BF16),
            pltpu.VMEM((2 * tq, tk), BF16),
            pltpu.VMEM((2 * tq, tk), BF16),
            pltpu.VMEM((2 * tq, tk), F32),
            pltpu.VMEM((2 * tq, tk), F32),
            pltpu.VMEM((2 * tq, tk), BF16),
            pltpu.VMEM((2 * tq, tk), BF16),
            pltpu.VMEM((nq + 1, 2 * tq, 1), F32),
            pltpu.VMEM((nq + 1, tq, LANES), F32),
        ],
    )
    return pl.pallas_call(
        functools.partial(_sb_prompt_kernel, tq=tq, tk=tk),
        grid_spec=grid_spec,
        out_shape=jax.ShapeDtypeStruct((N_PROMPT, SB_WIDTH), BF16),
        compiler_params=_params(40, ("arbitrary", "arbitrary")),
        name="sb_prompt",
    )(qi_tbl, kt_tbl, first_tbl, bias_parts, q_bf16, k_bf16, v_bf16)


def _sb_sample_kernel(pt_ref, q_ref, bias_ref, *refs):
    del pt_ref
    npg = PAGES_PER_STEP
    hp = SB_ROWS_PER_PAGE
    k_refs = refs[:npg]
    v_refs = refs[npg:2 * npg]
    o_ref = refs[2 * npg]
    qb_ref, z_ref, w_ref, acc_ref, carry_ref = refs[2 * npg + 1:]
    j = pl.program_id(1)
    nrow = npg * hp

    @pl.when(j == 0)
    def _():
        qb_ref[...] = jnp.broadcast_to(q_ref[0], (PAGE_SIZE, SB_WIDTH)).T
        acc_ref[...] = jnp.zeros_like(acc_ref)
        carry_ref[...] = jnp.zeros_like(carry_ref)
        z_ref[...] = jnp.zeros_like(z_ref)

    for p in range(npg):
        for h in range(SB_HEADS):
            ch = slice(h * SB_HEAD_DIM, (h + 1) * SB_HEAD_DIM)
            prod = k_refs[p][0, ch, :] * qb_ref[ch, :]
            z_ref[p * hp + h:p * hp + h + 1, :] = jnp.sum(prod, axis=0, keepdims=True)
    z = z_ref[...] + bias_ref[...]
    sp = _softplus(z)
    log_beta = z - sp
    hi, lo = _split_bf16(sp)
    jj = lax.broadcasted_iota(jnp.int32, (PAGE_SIZE, PAGE_SIZE), 0)
    ss = lax.broadcasted_iota(jnp.int32, (PAGE_SIZE, PAGE_SIZE), 1)
    later = jnp.where(jj > ss, 1.0, 0.0).astype(BF16)
    ones = jnp.ones((PAGE_SIZE, LANES), BF16)
    after = (jnp.dot(hi, later, preferred_element_type=F32)
             + jnp.dot(lo, later, preferred_element_type=F32))
    total = (jnp.dot(hi, ones, preferred_element_type=F32)
             + jnp.dot(lo, ones, preferred_element_type=F32))
    t_hi, t_lo = _split_bf16(total)
    rr = lax.broadcasted_iota(jnp.int32, (2 * nrow, nrow), 0)
    cc = lax.broadcasted_iota(jnp.int32, (2 * nrow, nrow), 1)
    same_head = (rr % hp) == (cc % hp)
    earlier = (cc // hp) < ((rr % nrow) // hp)
    pick = jnp.where(same_head & (earlier | (rr >= nrow)), 1.0, 0.0).astype(BF16)
    sums = (jnp.dot(pick, t_hi, preferred_element_type=F32)
            + jnp.dot(pick, t_lo, preferred_element_type=F32))
    carry = carry_ref[...]
    w_ref[...] = jnp.exp(log_beta - after - sums[0:nrow] - carry)
    carry_ref[...] = carry + sums[nrow:]

    for h in range(SB_HEADS):
        ch = slice(h * SB_HEAD_DIM, (h + 1) * SB_HEAD_DIM)
        acc = acc_ref[ch, :]
        for p in range(npg):
            acc = acc + v_refs[p][0, ch, :] * w_ref[p * hp + h:p * hp + h + 1, :]
        acc_ref[ch, :] = acc

    @pl.when(j == pl.num_programs(1) - 1)
    def _():
        o_ref[0] = jnp.sum(acc_ref[...].T, axis=0, keepdims=True)


def _sb_sample(q, bias, cache_kt, cache_vt, page_table):
    npg = PAGES_PER_STEP
    hp = SB_ROWS_PER_PAGE
    assert N_PAGES % npg == 0 and hp >= SB_HEADS and PAGE_SIZE == LANES
    steps = N_PAGES // npg
    head_bias = jnp.zeros((hp,), F32).at[:SB_HEADS].set(bias)
    bias_rows = jnp.broadcast_to(jnp.tile(head_bias, npg)[:, None], (npg * hp, PAGE_SIZE))

    def page_spec(p):
        return pl.BlockSpec(
            (1, SB_WIDTH, PAGE_SIZE),
            lambda b, j, pt: (pt[b, N_PAGES - 1 - (j * npg + p)], 0, 0))

    grid_spec = pltpu.PrefetchScalarGridSpec(
        num_scalar_prefetch=1,
        grid=(DEC_BATCH, steps),
        in_specs=[pl.BlockSpec((1, 1, SB_WIDTH), lambda b, j, pt: (b, 0, 0)),
                  pl.BlockSpec((npg * hp, PAGE_SIZE), lambda b, j, pt: (0, 0))]
                 + [page_spec(p) for p in range(npg)] + [page_spec(p) for p in range(npg)],
        out_specs=pl.BlockSpec((1, 1, SB_WIDTH), lambda b, j, pt: (b, 0, 0)),
        scratch_shapes=[pltpu.VMEM((SB_WIDTH, PAGE_SIZE), F32),
                        pltpu.VMEM((npg * hp, PAGE_SIZE), F32),
                        pltpu.VMEM((npg * hp, PAGE_SIZE), F32),
                        pltpu.VMEM((SB_WIDTH, PAGE_SIZE), F32),
                        pltpu.VMEM((npg * hp, PAGE_SIZE), F32)],
    )
    out = pl.pallas_call(
        _sb_sample_kernel,
        grid_spec=grid_spec,
        out_shape=jax.ShapeDtypeStruct((DEC_BATCH, 1, SB_WIDTH), F32),
        compiler_params=_params(52, ("arbitrary", "arbitrary")),
        name="sb_sample",
    )(page_table, q.reshape(DEC_BATCH, 1, SB_WIDTH), bias_rows,
      *([cache_kt] * npg), *([cache_vt] * npg))
    return out.reshape(DEC_BATCH, SB_WIDTH)


def kernel(x_prompt, x_sample, cache_k, cache_v, state_conv, cache_mem_k, cache_mem_v, page_table, mem_prompt, norm_mix_pre, norm_mix_post, norm_ffn_pre, norm_ffn_post, w_in_a, conv_w, conv_b, conv_ln_g, conv_ln_b, w_in_b, sb_bias, kv_norm_g, w_kv, mem_norm_g, w_mem_kv, w_out, w_ffn_up, w_ffn_down):
    hist = CONV_WIDTH - 1
    mk_p, mv_p = _memkv(mem_prompt.reshape(BATCH * N_MEM, D_MODEL), mem_norm_g,
                        jnp.transpose(w_mem_kv, (0, 2, 1)))
    mem_view = (DEPTH, DEC_BATCH, MEM_WIDTH, N_MEM)
    mk_s = jnp.transpose(cache_mem_k, (0, 1, 3, 4, 2)).reshape(mem_view)
    mv_s = jnp.transpose(cache_mem_v, (0, 1, 3, 4, 2)).reshape(mem_view)
    wup = w_ffn_up.astype(BF16)
    wdn = w_ffn_down.astype(BF16)
    cache_k3 = jnp.transpose(cache_k, (0, 2, 3, 1)).reshape(cache_k.shape[0], SB_WIDTH, PAGE_SIZE)
    cache_v3 = jnp.transpose(cache_v, (0, 2, 3, 1)).reshape(cache_v.shape[0], SB_WIDTH, PAGE_SIZE)

    xp = x_prompt.reshape(N_PROMPT, D_MODEL)
    xs = x_sample.reshape(DEC_BATCH, D_MODEL)
    conv_p, conv_s = [], []
    k_p = v_p = kb_p = vb_p = k_s = v_s = None
    for l in range(DEPTH):
        wo = w_out[l].astype(BF16)
        if l < N_A:
            w_in = w_in_a[l].astype(BF16)
            conv_params = (conv_w[l], conv_b[l], conv_ln_g[l], conv_ln_b[l])
            u_p, qm_p = _inproj(xp, norm_mix_pre[l], w_in, glu=True, main_dtype=F32, qm_dtype=BF16)
            u_s, qm_s = _inproj(xs, norm_mix_pre[l], w_in, glu=True, main_dtype=F32, qm_dtype=F32)
            xp = _mix_prompt(xp, u_p, qm_p, mk_p, mv_p, l, wo, norm_mix_post[l], conv_params)
            xs, st = _mix_sample(xs, u_s, qm_s, mk_s, mv_s, l, wo, norm_mix_post[l], conv_params,
                                 state_conv[l])
            conv_p.append(u_p.reshape(BATCH, SEQ, CONV_CH)[:, SEQ - hist:])
            conv_s.append(st)
        else:
            w_in = w_in_b[l - N_A].astype(BF16)
            bias = sb_bias[l - N_A]
            q_p, qm_p = _inproj(xp, norm_mix_pre[l], w_in, glu=False, main_dtype=BF16, qm_dtype=BF16)
            q_s, qm_s = _inproj(xs, norm_mix_pre[l], w_in, glu=False, main_dtype=F32, qm_dtype=F32)
            mix_p = _sb_prompt(q_p, kb_p, vb_p, bias)
            mix_s = _sb_sample(q_s, bias, cache_k3, cache_v3, page_table)
            xp = _mix_prompt(xp, mix_p, qm_p, mk_p, mv_p, l, wo, norm_mix_post[l])
            xs = _mix_sample(xs, mix_s, qm_s, mk_s, mv_s, l, wo, norm_mix_post[l])
        xp = _ffn(xp, norm_ffn_pre[l], wup, wdn, l, norm_ffn_post[l])
        xs = _ffn(xs, norm_ffn_pre[l], wup, wdn, l, norm_ffn_post[l])
        if l == N_A - 1:
            wkv = w_kv.astype(BF16)
            k_p, v_p, kb_p, vb_p = _kv(xp, kv_norm_g, wkv)
            k_s, v_s, _, _ = _kv(xs, kv_norm_g, wkv)

    def mem_out(t):
        t = t.reshape(DEPTH, BATCH, MEM_HEADS, MEM_HEAD_DIM, N_MEM)
        return jnp.transpose(t, (0, 1, 4, 2, 3))

    return (
        xp.reshape(BATCH, SEQ, D_MODEL),
        xs.reshape(DEC_BATCH, 1, D_MODEL),
        k_p.reshape(BATCH, SEQ, SB_HEADS, SB_HEAD_DIM),
        v_p.reshape(BATCH, SEQ, SB_HEADS, SB_HEAD_DIM),
        jnp.stack(conv_p),
        mem_out(mk_p),
        mem_out(mv_p),
        k_s.reshape(DEC_BATCH, 1, SB_HEADS, SB_HEAD_DIM),
        v_s.reshape(DEC_BATCH, 1, SB_HEADS, SB_HEAD_DIM),
        jnp.stack(conv_s),
    )
```
